```python
import jax, jax.numpy as jnp
from jax import lax
import numpy as np

D_MODEL = 2048
BATCH = 2
SEQ = 8192
DEPTH = 1

D_MIX = D_MODEL
CONV_WIDTH = D_MIX // 2
CONV_HEADS = 8
CONV_K = 3
POOL_WIDTH = D_MIX - CONV_WIDTH
POOL_WINDOWS = (2, 4, 8, 16)
POOL_GROUPS = len(POOL_WINDOWS)
POOL_GROUP_DIM = POOL_WIDTH // POOL_GROUPS
IN_PROJ_WIDTH = 3 * CONV_WIDTH + POOL_WIDTH
PEER_HEADS = 8
PEER_N_KEYS = 128
PEER_N_EXPERTS = PEER_N_KEYS * PEER_N_KEYS
PEER_TOPK = 16
PEER_D_KEY = 256
PEER_HALF = PEER_D_KEY // 2
PEER_CHUNK = 128
RMS_EPS = 1e-6

kernel_name = "hybrid_conv_pool_peer_block"


def rmsnorm(x, g):
    xf = x.astype(jnp.float32)
    y = xf * lax.rsqrt(jnp.mean(xf * xf, axis=-1, keepdims=True) + RMS_EPS)
    return (y * g.astype(jnp.float32)).astype(x.dtype)


def causal_short_conv(z, w, b):
    seq = z.shape[1]
    zp = jnp.pad(z, ((0, 0), (CONV_K - 1, 0), (0, 0)))
    y = b
    for k in range(CONV_K):
        y = y + w[k] * zp[:, k:k + seq, :]
    return y


def causal_multiscale_pool(z):
    bsz, seq, _ = z.shape
    zg = z.astype(jnp.float32).reshape(bsz, seq, POOL_GROUPS, POOL_GROUP_DIM)
    cs0 = jnp.concatenate([jnp.zeros((bsz, 1, POOL_GROUPS, POOL_GROUP_DIM), jnp.float32),
                           jnp.cumsum(zg, axis=1)], axis=1)
    pos = jnp.arange(1, seq + 1, dtype=jnp.float32)
    outs = []
    for g, w in enumerate(POOL_WINDOWS):
        c = cs0[:, :, g, :]
        upper = c[:, 1:, :]
        lower = jnp.concatenate([jnp.zeros((bsz, w - 1, POOL_GROUP_DIM), jnp.float32),
                                 c[:, :seq - w + 1, :]], axis=1)
        count = jnp.minimum(pos, float(w))[None, :, None]
        outs.append((upper - lower) / count - zg[:, :, g, :])
    return jnp.stack(outs, axis=2).astype(z.dtype)


def peer_ffn(h, w_q, sub_keys, expert_u, expert_v):
    bsz, seq, d = h.shape
    t = bsz * seq
    ht = h.reshape(t, d)
    q = (ht @ w_q).reshape(t, PEER_HEADS, 2, PEER_HALF).astype(jnp.float32)
    scores = jnp.einsum('thpd,hpkd->thpk', q, sub_keys.astype(jnp.float32))
    sv, si = lax.top_k(scores, PEER_TOPK)
    cand = (sv[:, :, 0, :, None] + sv[:, :, 1, None, :]).reshape(t, PEER_HEADS, PEER_TOPK * PEER_TOPK)
    cand_idx = (si[:, :, 0, :, None] * PEER_N_KEYS + si[:, :, 1, None, :]).reshape(t, PEER_HEADS, PEER_TOPK * PEER_TOPK)
    top_s, top_pos = lax.top_k(cand, PEER_TOPK)
    eidx = jnp.take_along_axis(cand_idx, top_pos, axis=-1)
    gates = jax.nn.softmax(top_s, axis=-1)
    n_chunks = t // PEER_CHUNK
    hk = PEER_HEADS * PEER_TOPK
    xs = ht.reshape(n_chunks, PEER_CHUNK, d)
    es = eidx.reshape(n_chunks, PEER_CHUNK, hk)
    gs = gates.astype(h.dtype).reshape(n_chunks, PEER_CHUNK, hk)

    def expert_chunk(args):
        xc, ec, gc = args
        u = jnp.take(expert_u, ec, axis=0)
        act = jax.nn.gelu(jnp.einsum('cd,ckd->ck', xc, u))
        v = jnp.take(expert_v, ec, axis=0)
        return jnp.einsum('ck,ckd->cd', gc * act, v)

    out = lax.map(expert_chunk, (xs, es, gs))
    return out.reshape(bsz, seq, d)


def setup_inputs(seed: int = 0) -> dict:
    key = jax.random.key(seed)
    ks = jax.random.split(key, 16)
    f32 = jnp.float32
    x = jax.random.normal(ks[0], (BATCH, SEQ, D_MODEL), f32)
    norm_mix = 1.0 + 0.01 * jax.random.normal(ks[1], (DEPTH, D_MODEL), f32)
    w_in = jax.random.normal(ks[2], (DEPTH, D_MODEL, IN_PROJ_WIDTH), f32) * D_MODEL ** -0.5
    conv_w = jax.random.normal(ks[3], (DEPTH, CONV_K, CONV_WIDTH), f32) * CONV_K ** -0.5
    conv_b = 0.02 * jax.random.normal(ks[4], (DEPTH, CONV_WIDTH), f32)
    pool_w = jax.random.normal(ks[5], (DEPTH, POOL_GROUPS, POOL_GROUP_DIM, POOL_GROUP_DIM), f32) * POOL_GROUP_DIM ** -0.5
    pool_scale = 1.0 + 0.1 * jax.random.normal(ks[6], (DEPTH, POOL_WIDTH), f32)
    w_out = jax.random.normal(ks[7], (DEPTH, D_MIX, D_MODEL), f32) * D_MIX ** -0.5
    norm_ffn = 1.0 + 0.01 * jax.random.normal(ks[8], (DEPTH, D_MODEL), f32)
    peer_w_q = jax.random.normal(ks[9], (DEPTH, D_MODEL, PEER_HEADS * PEER_D_KEY), f32) * D_MODEL ** -0.5
    peer_sub_keys = jax.random.normal(ks[10], (DEPTH, PEER_HEADS, 2, PEER_N_KEYS, PEER_HALF), f32) * PEER_HALF ** -0.5
    peer_u = jax.random.normal(ks[11], (DEPTH, PEER_N_EXPERTS, D_MODEL), f32) * D_MODEL ** -0.5
    peer_v = jax.random.normal(ks[12], (DEPTH, PEER_N_EXPERTS, D_MODEL), f32) * 0.5
    norm_final = 1.0 + 0.01 * jax.random.normal(ks[13], (D_MODEL,), f32)
    return {"x": x, "norm_mix": norm_mix, "w_in": w_in, "conv_w": conv_w, "conv_b": conv_b,
            "pool_w": pool_w, "pool_scale": pool_scale, "w_out": w_out, "norm_ffn": norm_ffn,
            "peer_w_q": peer_w_q, "peer_sub_keys": peer_sub_keys, "peer_u": peer_u,
            "peer_v": peer_v, "norm_final": norm_final}


def reference(x, norm_mix, w_in, conv_w, conv_b, pool_w, pool_scale, w_out, norm_ffn,
              peer_w_q, peer_sub_keys, peer_u, peer_v, norm_final):
    bsz, seq, _ = x.shape
    for l in range(DEPTH):
        h = rmsnorm(x, norm_mix[l])
        proj = h @ w_in[l]
        gate_b = proj[..., :CONV_WIDTH]
        gate_c = proj[..., CONV_WIDTH:2 * CONV_WIDTH]
        val = proj[..., 2 * CONV_WIDTH:3 * CONV_WIDTH]
        pool_in = proj[..., 3 * CONV_WIDTH:]
        y_conv = gate_b * causal_short_conv(gate_c * val, conv_w[l], conv_b[l])
        pooled = causal_multiscale_pool(pool_in)
        y_pool = jnp.einsum('bsgc,gcd->bsgd', pooled, pool_w[l]).reshape(bsz, seq, POOL_WIDTH) * pool_scale[l]
        mix = jnp.concatenate([y_conv, y_pool], axis=-1) @ w_out[l]
        x = x + mix
        h2 = rmsnorm(x, norm_ffn[l])
        x = x + peer_ffn(h2, peer_w_q[l], peer_sub_keys[l], peer_u[l], peer_v[l])
    return rmsnorm(x, norm_final)
```

```python
import functools

import jax
import jax.numpy as jnp
from jax import lax
from jax.experimental import pallas as pl
from jax.experimental.pallas import tpu as pltpu

CONV_K = 3
POOL_WINDOWS = (2, 4, 8, 16)
PEER_HEADS = 8
PEER_N_KEYS = 128
PEER_TOPK = 16
PEER_HALF = 128
RMS_EPS = 1e-6

HALO = 16
NEG_INF = float("-inf")

_BF16 = jnp.bfloat16
_F32 = jnp.float32


def _rmsnorm(xf, g):
    ms = jnp.mean(xf * xf, axis=-1, keepdims=True)
    return xf * lax.rsqrt(ms + RMS_EPS) * g


def _gelu_tanh(x):
    return 0.5 * x * (1.0 + jnp.tanh(0.7978845608028654 * (x + 0.044715 * (x * x * x))))


def _in_proj_body(x_ref, g_ref, w_ref, o_ref, h_scr):
    @pl.when(pl.program_id(1) == 0)
    def _():
        h_scr[...] = _rmsnorm(x_ref[...], g_ref[...]).astype(_BF16)

    o_ref[...] = jnp.dot(h_scr[...], w_ref[...], preferred_element_type=_F32)


def _in_proj(x2, g, w_bf16, tm, tn):
    t, d = x2.shape
    n = w_bf16.shape[1]
    return pl.pallas_call(
        _in_proj_body,
        grid=(t // tm, n // tn),
        in_specs=[
            pl.BlockSpec((tm, d), lambda i, j: (i, 0)),
            pl.BlockSpec((1, d), lambda i, j: (0, 0)),
            pl.BlockSpec((d, tn), lambda i, j: (0, j)),
        ],
        out_specs=pl.BlockSpec((tm, tn), lambda i, j: (i, j)),
        out_shape=jax.ShapeDtypeStruct((t, n), _F32),
        scratch_shapes=[pltpu.VMEM((tm, d), _BF16)],
        compiler_params=pltpu.CompilerParams(
            dimension_semantics=("arbitrary", "arbitrary"),
            vmem_limit_bytes=40 * 1024 * 1024),
        name="in_proj",
    )(x2, g, w_bf16)


def _mixer_body(proj_ref, halo_ref, x_ref, cw_ref, cb_ref, pw_ref, psc_ref, wo_ref, g2_ref,
                x1_ref, h2_ref, zs, ps, ycat, *, tm, tiles_per_seq, cw, gd):
    tile_in_seq = pl.program_id(0) % tiles_per_seq
    first = tile_in_seq == 0

    zh = halo_ref[:, cw:2 * cw] * halo_ref[:, 2 * cw:3 * cw]
    zs[0:HALO, :] = jnp.where(first, 0.0, zh)
    zs[HALO:, :] = proj_ref[:, cw:2 * cw] * proj_ref[:, 2 * cw:3 * cw]
    ps[0:HALO, :] = jnp.where(first, 0.0, halo_ref[:, 3 * cw:])
    ps[HALO:, :] = proj_ref[:, 3 * cw:]

    y = cb_ref[...]
    for k in range(CONV_K):
        off = HALO - (CONV_K - 1) + k
        y = y + cw_ref[k:k + 1, :] * zs[off:off + tm, :]
    ycat[:, 0:cw] = (proj_ref[:, 0:cw] * y).astype(_BF16)

    pos = tile_in_seq * tm + lax.broadcasted_iota(jnp.int32, (tm, 1), 0)
    for g, w in enumerate(POOL_WINDOWS):
        c0 = g * gd
        cur = ps[HALO:HALO + tm, c0:c0 + gd]
        acc = cur
        for k in range(1, w):
            acc = acc + ps[HALO - k:HALO - k + tm, c0:c0 + gd]
        count = jnp.minimum(pos + 1, w).astype(_F32)
        pooled = acc / count - cur
        yp = jnp.dot(pooled.astype(_BF16), pw_ref[g], preferred_element_type=_F32)
        ycat[:, cw + c0:cw + c0 + gd] = (yp * psc_ref[:, c0:c0 + gd]).astype(_BF16)

    x1 = x_ref[...] + jnp.dot(ycat[...], wo_ref[...], preferred_element_type=_F32)
    x1_ref[...] = x1
    h2_ref[...] = _rmsnorm(x1, g2_ref[...])


def _mixer(proj, x2, conv_w, conv_b, pool_w_bf16, pool_scale, w_out_bf16, g2, seq, tm):
    t, d = x2.shape
    n = proj.shape[1]
    cw = conv_w.shape[1]
    gd = pool_w_bf16.shape[1]
    hb = tm // HALO
    body = functools.partial(_mixer_body, tm=tm, tiles_per_seq=seq // tm, cw=cw, gd=gd)
    return pl.pallas_call(
        body,
        grid=(t // tm,),
        in_specs=[
            pl.BlockSpec((tm, n), lambda i: (i, 0)),
            pl.BlockSpec((HALO, n), lambda i: (jnp.maximum(i * hb - 1, 0), 0)),
            pl.BlockSpec((tm, d), lambda i: (i, 0)),
            pl.BlockSpec((CONV_K, cw), lambda i: (0, 0)),
            pl.BlockSpec((1, cw), lambda i: (0, 0)),
            pl.BlockSpec((len(POOL_WINDOWS), gd, gd), lambda i: (0, 0, 0)),
            pl.BlockSpec((1, cw), lambda i: (0, 0)),
            pl.BlockSpec((d, d), lambda i: (0, 0)),
            pl.BlockSpec((1, d), lambda i: (0, 0)),
        ],
        out_specs=[pl.BlockSpec((tm, d), lambda i: (i, 0)),
                   pl.BlockSpec((tm, d), lambda i: (i, 0))],
        out_shape=[jax.ShapeDtypeStruct((t, d), _F32), jax.ShapeDtypeStruct((t, d), _F32)],
        scratch_shapes=[pltpu.VMEM((HALO + tm, cw), _F32),
                        pltpu.VMEM((HALO + tm, cw), _F32),
                        pltpu.VMEM((tm, d), _BF16)],
        compiler_params=pltpu.CompilerParams(
            dimension_semantics=("arbitrary",),
            vmem_limit_bytes=52 * 1024 * 1024),
        name="mixer",
    )(proj, proj, x2, conv_w, conv_b, pool_w_bf16, pool_scale, w_out_bf16, g2)


def _topk_rows(s, k):
    n = s.shape[0]
    iota = lax.broadcasted_iota(jnp.int32, s.shape, 0)
    vals, idxs = [], []
    for _ in range(k):
        m = jnp.max(s, axis=0, keepdims=True)
        i = jnp.min(jnp.where(s == m, iota, n), axis=0, keepdims=True)
        vals.append(m)
        idxs.append(i)
        s = jnp.where(iota == i, NEG_INF, s)
    return jnp.concatenate(vals, axis=0), jnp.concatenate(idxs, axis=0)


def _select_rows(table, sel, n):
    out = jnp.zeros(sel.shape, table.dtype)
    for i in range(n):
        out = jnp.where(sel == i, table[i:i + 1, :], out)
    return out


def _retrieval_body(h_ref, wq_ref, sk_ref, eidx_ref, gate_ref):
    q = jnp.dot(h_ref[...].astype(_BF16), wq_ref[...], preferred_element_type=_F32)
    nt = (((1,), (1,)), ((), ()))
    for h in range(PEER_HEADS):
        sv, si = [], []
        for p in range(2):
            hp = 2 * h + p
            qhp = q[:, hp * PEER_HALF:(hp + 1) * PEER_HALF].astype(_BF16)
            s = lax.dot_general(sk_ref[hp], qhp, nt, preferred_element_type=_F32)
            v, i = _topk_rows(s, PEER_TOPK)
            sv.append(v)
            si.append(i)
        cand = jnp.concatenate([sv[0][i:i + 1, :] + sv[1] for i in range(PEER_TOPK)], axis=0)
        top_s, top_pos = _topk_rows(cand, PEER_TOPK)
        e0 = _select_rows(si[0], top_pos // PEER_TOPK, PEER_TOPK)
        e1 = _select_rows(si[1], top_pos % PEER_TOPK, PEER_TOPK)
        ex = jnp.exp(top_s - jnp.max(top_s, axis=0, keepdims=True))
        r0 = h * PEER_TOPK
        eidx_ref[r0:r0 + PEER_TOPK, :] = e0 * PEER_N_KEYS + e1
        gate_ref[r0:r0 + PEER_TOPK, :] = ex / jnp.sum(ex, axis=0, keepdims=True)


def _retrieval(h2, wq_bf16, sk_bf16, tm):
    t, d = h2.shape
    hk = PEER_HEADS * PEER_TOPK
    return pl.pallas_call(
        _retrieval_body,
        grid=(t // tm,),
        in_specs=[
            pl.BlockSpec((tm, d), lambda i: (i, 0)),
            pl.BlockSpec(wq_bf16.shape, lambda i: (0, 0)),
            pl.BlockSpec(sk_bf16.shape, lambda i: (0, 0, 0)),
        ],
        out_specs=[pl.BlockSpec((hk, tm), lambda i: (0, i)),
                   pl.BlockSpec((hk, tm), lambda i: (0, i))],
        out_shape=[jax.ShapeDtypeStruct((hk, t), jnp.int32), jax.ShapeDtypeStruct((hk, t), _F32)],
        compiler_params=pltpu.CompilerParams(
            dimension_semantics=("arbitrary",),
            vmem_limit_bytes=40 * 1024 * 1024),
        name="retrieval",
    )(h2, wq_bf16, sk_bf16)


def _pack_body(u_ref, v_ref, o_ref):
    d = u_ref.shape[1]
    o_ref[:, 0:d] = u_ref[...]
    o_ref[:, d:] = v_ref[...]


def _pack_experts(u, v, rows):
    n, d = u.shape
    return pl.pallas_call(
        _pack_body,
        grid=(n // rows,),
        in_specs=[pl.BlockSpec((rows, d), lambda i: (i, 0)),
                  pl.BlockSpec((rows, d), lambda i: (i, 0))],
        out_specs=pl.BlockSpec((rows, 2 * d), lambda i: (i, 0)),
        out_shape=jax.ShapeDtypeStruct((n, 2 * d), u.dtype),
        compiler_params=pltpu.CompilerParams(dimension_semantics=("arbitrary",)),
        name="pack_experts",
    )(u, v)


def _peer_body(idx_ref, h_ref, x1_ref, gate_ref, gf_ref, uv_hbm, o_ref, buf, y_scr, sem, *, tt, hk, d):
    def row_copy(e, slot, k):
        return pltpu.make_async_copy(uv_hbm.at[pl.ds(e, 1), :], buf.at[slot, pl.ds(k, 1), :], sem.at[slot])

    def issue(t, slot):
        for k in range(hk):
            row_copy(idx_ref[t * hk + k], slot, k).start()

    def wait_all(slot):
        pltpu.make_async_copy(uv_hbm.at[pl.ds(0, hk), :], buf.at[slot], sem.at[slot]).wait()

    lane_tok = lax.broadcasted_iota(jnp.int32, (hk, tt), 1)

    issue(0, 0)

    def step(t, carry):
        slot = t & 1

        @pl.when(t + 1 < tt)
        def _():
            issue(t + 1, 1 - slot)

        wait_all(slot)
        x = h_ref[pl.ds(t, 1), :]
        a = jnp.sum(buf[slot, :, 0:d] * x, axis=1, keepdims=True)
        g = jnp.sum(jnp.where(lane_tok == t, gate_ref[...], 0.0), axis=1, keepdims=True)
        c = g * _gelu_tanh(a)
        y_scr[pl.ds(t, 1), :] = jnp.sum(c * buf[slot, :, d:], axis=0, keepdims=True)
        return carry

    lax.fori_loop(0, tt, step, 0)
    o_ref[...] = _rmsnorm(x1_ref[...] + y_scr[...], gf_ref[...])


def _peer(eidx_flat, h2, x1, gates_t, g_final, uv, tt):
    t, d = h2.shape
    hk = gates_t.shape[0]
    body = functools.partial(_peer_body, tt=tt, hk=hk, d=d)
    return pl.pallas_call(
        body,
        grid=(t // tt,),
        in_specs=[
            pl.BlockSpec((tt * hk,), lambda i: (i,), memory_space=pltpu.SMEM),
            pl.BlockSpec((tt, d), lambda i: (i, 0)),
            pl.BlockSpec((tt, d), lambda i: (i, 0)),
            pl.BlockSpec((hk, tt), lambda i: (0, i)),
            pl.BlockSpec((1, d), lambda i: (0, 0)),
            pl.BlockSpec(memory_space=pl.ANY),
        ],
        out_specs=pl.BlockSpec((tt, d), lambda i: (i, 0)),
        out_shape=jax.ShapeDtypeStruct((t, d), _F32),
        scratch_shapes=[pltpu.VMEM((2, hk, 2 * d), _F32),
                        pltpu.VMEM((tt, d), _F32),
                        pltpu.SemaphoreType.DMA((2,))],
        compiler_params=pltpu.CompilerParams(
            dimension_semantics=("arbitrary",),
            vmem_limit_bytes=40 * 1024 * 1024),
        name="peer",
    )(eidx_flat, h2, x1, gates_t, g_final, uv)


def kernel(x, norm_mix, w_in, conv_w, conv_b, pool_w, pool_scale, w_out, norm_ffn,
           peer_w_q, peer_sub_keys, peer_u, peer_v, norm_final):
    bsz, seq, d = x.shape
    t = bsz * seq
    depth = w_in.shape[0]
    assert depth == 1, "the final rmsnorm is fused into the single layer's peer call"
    xt = x.reshape(t, d)
    tm_proj = min(512, t)
    tm_mix = min(256, seq)
    tm_ret = min(256, t)
    tt = 128
    for l in range(depth):
        proj = _in_proj(xt, norm_mix[l][None, :], w_in[l].astype(_BF16), tm_proj, 1024)
        x1, h2 = _mixer(proj, xt, conv_w[l], conv_b[l][None, :], pool_w[l].astype(_BF16),
                        pool_scale[l][None, :], w_out[l].astype(_BF16), norm_ffn[l][None, :],
                        seq, tm_mix)
        sk = peer_sub_keys[l].reshape(2 * PEER_HEADS, PEER_N_KEYS, PEER_HALF).astype(_BF16)
        eidx_t, gates_t = _retrieval(h2, peer_w_q[l].astype(_BF16), sk, tm_ret)
        uv = _pack_experts(peer_u[l], peer_v[l], 256)
        xt = _peer(eidx_t.T.reshape(-1), h2, x1, gates_t, norm_final[None, :], uv, tt)
    return xt.reshape(bsz, seq, d)
```

```python
import functools

import jax
import jax.numpy as jnp
from jax import lax
from jax.experimental import pallas as pl
from jax.experimental.pallas import tpu as pltpu

CONV_K = 3
POOL_WINDOWS = (2, 4, 8, 16)
PEER_HEADS = 8
PEER_N_KEYS = 128
PEER_TOPK = 16
PEER_HALF = 128
RMS_EPS = 1e-6

HALO = 16
NEG_INF = float("-inf")

_BF16 = jnp.bfloat16
_F32 = jnp.float32
_PACKED = jnp.uint32


def _rmsnorm(xf, g):
    ms = jnp.mean(xf * xf, axis=-1, keepdims=True)
    return xf * lax.rsqrt(ms + RMS_EPS) * g


def _gelu_tanh(x):
    return 0.5 * x * (1.0 + jnp.tanh(0.7978845608028654 * (x + 0.044715 * (x * x * x))))


def _in_proj_body(x_ref, g_ref, w_ref, o_ref, h_scr):
    @pl.when(pl.program_id(1) == 0)
    def _():
        h_scr[...] = _rmsnorm(x_ref[...], g_ref[...]).astype(_BF16)

    o_ref[...] = jnp.dot(h_scr[...], w_ref[...], preferred_element_type=_F32)


def _in_proj(x2, g, w_bf16, tm, tn):
    t, d = x2.shape
    n = w_bf16.shape[1]
    return pl.pallas_call(
        _in_proj_body,
        grid=(t // tm, n // tn),
        in_specs=[
            pl.BlockSpec((tm, d), lambda i, j: (i, 0)),
            pl.BlockSpec((1, d), lambda i, j: (0, 0)),
            pl.BlockSpec((d, tn), lambda i, j: (0, j)),
        ],
        out_specs=pl.BlockSpec((tm, tn), lambda i, j: (i, j)),
        out_shape=jax.ShapeDtypeStruct((t, n), _F32),
        scratch_shapes=[pltpu.VMEM((tm, d), _BF16)],
        compiler_params=pltpu.CompilerParams(
            dimension_semantics=("arbitrary", "arbitrary"),
            vmem_limit_bytes=40 * 1024 * 1024),
        name="in_proj",
    )(x2, g, w_bf16)


def _mixer_body(proj_ref, halo_ref, x_ref, cw_ref, cb_ref, pw_ref, psc_ref, wo_ref, g2_ref,
                x1_ref, h2_ref, zs, ps, ycat, *, tm, tiles_per_seq, cw, gd):
    tile_in_seq = pl.program_id(0) % tiles_per_seq
    first = tile_in_seq == 0

    zh = halo_ref[:, cw:2 * cw] * halo_ref[:, 2 * cw:3 * cw]
    zs[0:HALO, :] = jnp.where(first, 0.0, zh)
    zs[HALO:, :] = proj_ref[:, cw:2 * cw] * proj_ref[:, 2 * cw:3 * cw]
    ps[0:HALO, :] = jnp.where(first, 0.0, halo_ref[:, 3 * cw:])
    ps[HALO:, :] = proj_ref[:, 3 * cw:]

    y = cb_ref[...]
    for k in range(CONV_K):
        off = HALO - (CONV_K - 1) + k
        y = y + cw_ref[k:k + 1, :] * zs[off:off + tm, :]
    ycat[:, 0:cw] = (proj_ref[:, 0:cw] * y).astype(_BF16)

    pos = tile_in_seq * tm + lax.broadcasted_iota(jnp.int32, (tm, 1), 0)
    for g, w in enumerate(POOL_WINDOWS):
        c0 = g * gd
        cur = ps[HALO:HALO + tm, c0:c0 + gd]
        acc = cur
        for k in range(1, w):
            acc = acc + ps[HALO - k:HALO - k + tm, c0:c0 + gd]
        count = jnp.minimum(pos + 1, w).astype(_F32)
        pooled = acc / count - cur
        yp = jnp.dot(pooled.astype(_BF16), pw_ref[g], preferred_element_type=_F32)
        ycat[:, cw + c0:cw + c0 + gd] = (yp * psc_ref[:, c0:c0 + gd]).astype(_BF16)

    x1 = x_ref[...] + jnp.dot(ycat[...], wo_ref[...], preferred_element_type=_F32)
    x1_ref[...] = x1
    h2_ref[...] = _rmsnorm(x1, g2_ref[...])


def _mixer(proj, x2, conv_w, conv_b, pool_w_bf16, pool_scale, w_out_bf16, g2, seq, tm):
    t, d = x2.shape
    n = proj.shape[1]
    cw = conv_w.shape[1]
    gd = pool_w_bf16.shape[1]
    hb = tm // HALO
    body = functools.partial(_mixer_body, tm=tm, tiles_per_seq=seq // tm, cw=cw, gd=gd)
    return pl.pallas_call(
        body,
        grid=(t // tm,),
        in_specs=[
            pl.BlockSpec((tm, n), lambda i: (i, 0)),
            pl.BlockSpec((HALO, n), lambda i: (jnp.maximum(i * hb - 1, 0), 0)),
            pl.BlockSpec((tm, d), lambda i: (i, 0)),
            pl.BlockSpec((CONV_K, cw), lambda i: (0, 0)),
            pl.BlockSpec((1, cw), lambda i: (0, 0)),
            pl.BlockSpec((len(POOL_WINDOWS), gd, gd), lambda i: (0, 0, 0)),
            pl.BlockSpec((1, cw), lambda i: (0, 0)),
            pl.BlockSpec((d, d), lambda i: (0, 0)),
            pl.BlockSpec((1, d), lambda i: (0, 0)),
        ],
        out_specs=[pl.BlockSpec((tm, d), lambda i: (i, 0)),
                   pl.BlockSpec((tm, d), lambda i: (i, 0))],
        out_shape=[jax.ShapeDtypeStruct((t, d), _F32), jax.ShapeDtypeStruct((t, d), _F32)],
        scratch_shapes=[pltpu.VMEM((HALO + tm, cw), _F32),
                        pltpu.VMEM((HALO + tm, cw), _F32),
                        pltpu.VMEM((tm, d), _BF16)],
        compiler_params=pltpu.CompilerParams(
            dimension_semantics=("arbitrary",),
            vmem_limit_bytes=52 * 1024 * 1024),
        name="mixer",
    )(proj, proj, x2, conv_w, conv_b, pool_w_bf16, pool_scale, w_out_bf16, g2)


def _topk_rows(s, k):
    n = s.shape[0]
    iota = lax.broadcasted_iota(jnp.int32, s.shape, 0)
    vals, idxs = [], []
    for _ in range(k):
        m = jnp.max(s, axis=0, keepdims=True)
        i = jnp.min(jnp.where(s == m, iota, n), axis=0, keepdims=True)
        vals.append(m)
        idxs.append(i)
        s = jnp.where(iota == i, NEG_INF, s)
    return jnp.concatenate(vals, axis=0), jnp.concatenate(idxs, axis=0)


def _select_rows(table, sel):
    out = jnp.zeros(sel.shape, table.dtype)
    for i in range(table.shape[0]):
        out = jnp.where(sel == i, table[i:i + 1, :], out)
    return out


_PAIRS = [(i, j) for i in range(PEER_TOPK) for j in range(PEER_TOPK) if (i + 1) * (j + 1) <= PEER_TOPK]
_PAIR_ROWS = -(-len(_PAIRS) // 8) * 8


def _retrieval_body(h_ref, wq_ref, sk_ref, eidx_ref, gate_ref):
    q = jnp.dot(h_ref[...].astype(_BF16), wq_ref[...], preferred_element_type=_F32)
    nt = (((1,), (1,)), ((), ()))
    tm = q.shape[0]
    pad = _PAIR_ROWS - len(_PAIRS)
    for h in range(PEER_HEADS):
        sv, si = [], []
        for p in range(2):
            hp = 2 * h + p
            qhp = q[:, hp * PEER_HALF:(hp + 1) * PEER_HALF].astype(_BF16)
            s = lax.dot_general(sk_ref[hp], qhp, nt, preferred_element_type=_F32)
            v, i = _topk_rows(s, PEER_TOPK)
            sv.append(v)
            si.append(i)
        cand = jnp.concatenate([sv[0][i:i + 1, :] + sv[1][j:j + 1, :] for i, j in _PAIRS]
                               + [jnp.full((pad, tm), NEG_INF, _F32)], axis=0)
        cidx = jnp.concatenate([si[0][i:i + 1, :] * PEER_N_KEYS + si[1][j:j + 1, :] for i, j in _PAIRS]
                               + [jnp.zeros((pad, tm), jnp.int32)], axis=0)
        top_s, top_row = _topk_rows(cand, PEER_TOPK)
        ex = jnp.exp(top_s - jnp.max(top_s, axis=0, keepdims=True))
        r0 = h * PEER_TOPK
        eidx_ref[r0:r0 + PEER_TOPK, :] = _select_rows(cidx, top_row)
        gate_ref[r0:r0 + PEER_TOPK, :] = ex / jnp.sum(ex, axis=0, keepdims=True)


def _retrieval(h2, wq_bf16, sk_bf16, tm):
    t, d = h2.shape
    hk = PEER_HEADS * PEER_TOPK
    return pl.pallas_call(
        _retrieval_body,
        grid=(t // tm,),
        in_specs=[
            pl.BlockSpec((tm, d), lambda i: (i, 0)),
            pl.BlockSpec(wq_bf16.shape, lambda i: (0, 0)),
            pl.BlockSpec(sk_bf16.shape, lambda i: (0, 0, 0)),
        ],
        out_specs=[pl.BlockSpec((hk, tm), lambda i: (0, i)),
                   pl.BlockSpec((hk, tm), lambda i: (0, i))],
        out_shape=[jax.ShapeDtypeStruct((hk, t), jnp.int32), jax.ShapeDtypeStruct((hk, t), _F32)],
        compiler_params=pltpu.CompilerParams(
            dimension_semantics=("arbitrary",),
            vmem_limit_bytes=40 * 1024 * 1024),
        name="retrieval",
    )(h2, wq_bf16, sk_bf16)


def _pack_body(u_ref, v_ref, o_ref):
    o_ref[...] = pltpu.pack_elementwise([u_ref[...], v_ref[...]], packed_dtype=_BF16)


def _unpack(w, half):
    return pltpu.unpack_elementwise(w, index=half, packed_dtype=_BF16, unpacked_dtype=_F32)


def _pack_experts(u, v, rows):
    n, d = u.shape
    return pl.pallas_call(
        _pack_body,
        grid=(n // rows,),
        in_specs=[pl.BlockSpec((rows, d), lambda i: (i, 0)),
                  pl.BlockSpec((rows, d), lambda i: (i, 0))],
        out_specs=pl.BlockSpec((rows, d), lambda i: (i, 0)),
        out_shape=jax.ShapeDtypeStruct((n, d), _PACKED),
        compiler_params=pltpu.CompilerParams(dimension_semantics=("arbitrary",)),
        name="pack_experts",
    )(u, v)


PEER_SLOTS = 4


def _peer_body(idx_ref, h_ref, x1_ref, gate_ref, gf_ref, uv_hbm, o_ref, *scratch, tt, hk, d):
    bufs, (y_scr, sem) = scratch[:PEER_SLOTS], scratch[PEER_SLOTS:]
    ahead = PEER_SLOTS - 1

    def issue(t, slot):
        for k in range(hk):
            e = idx_ref[t * hk + k]
            pltpu.make_async_copy(uv_hbm.at[pl.ds(e, 1), :], bufs[slot].at[pl.ds(k, 1), :],
                                  sem.at[slot]).start(priority=k % 2)

    def wait_all(slot):
        pltpu.make_async_copy(uv_hbm.at[pl.ds(0, hk), :], bufs[slot], sem.at[slot]).wait()

    lane_tok = lax.broadcasted_iota(jnp.int32, (hk, tt), 1)

    def compute(t, slot):
        x = h_ref[pl.ds(t, 1), :]
        a = jnp.sum(_unpack(bufs[slot][...], 0) * x, axis=1, keepdims=True)
        g = jnp.sum(jnp.where(lane_tok == t, gate_ref[...], 0.0), axis=1, keepdims=True)
        c = g * _gelu_tanh(a)
        y_scr[pl.ds(t, 1), :] = jnp.sum(c * _unpack(bufs[slot][...], 1), axis=0, keepdims=True)

    def group(t0, n_issue):
        for s in range(PEER_SLOTS):
            wait_all(s)
            if s < n_issue:
                issue(t0 + s + ahead, (s + ahead) % PEER_SLOTS)
            compute(t0 + s, s)

    for s in range(ahead):
        issue(s, s)

    def steady(j, carry):
        group(PEER_SLOTS * j, PEER_SLOTS)
        return carry

    lax.fori_loop(0, tt // PEER_SLOTS - 1, steady, 0)
    group(tt - PEER_SLOTS, PEER_SLOTS - ahead)
    o_ref[...] = _rmsnorm(x1_ref[...] + y_scr[...], gf_ref[...])


def _peer(eidx_flat, h2, x1, gates_t, g_final, uv, tt):
    t, d = h2.shape
    hk = gates_t.shape[0]
    body = functools.partial(_peer_body, tt=tt, hk=hk, d=d)
    return pl.pallas_call(
        body,
        grid=(t // tt,),
        in_specs=[
            pl.BlockSpec((tt * hk,), lambda i: (i,), memory_space=pltpu.SMEM),
            pl.BlockSpec((tt, d), lambda i: (i, 0)),
            pl.BlockSpec((tt, d), lambda i: (i, 0)),
            pl.BlockSpec((hk, tt), lambda i: (0, i)),
            pl.BlockSpec((1, d), lambda i: (0, 0)),
            pl.BlockSpec(memory_space=pl.ANY),
        ],
        out_specs=pl.BlockSpec((tt, d), lambda i: (i, 0)),
        out_shape=jax.ShapeDtypeStruct((t, d), _F32),
        scratch_shapes=[pltpu.VMEM((hk, d), _PACKED)] * PEER_SLOTS
                       + [pltpu.VMEM((tt, d), _F32), pltpu.SemaphoreType.DMA((PEER_SLOTS,))],
        compiler_params=pltpu.CompilerParams(
            dimension_semantics=("arbitrary",),
            vmem_limit_bytes=40 * 1024 * 1024),
        name="peer",
    )(eidx_flat, h2, x1, gates_t, g_final, uv)


def kernel(x, norm_mix, w_in, conv_w, conv_b, pool_w, pool_scale, w_out, norm_ffn,
           peer_w_q, peer_sub_keys, peer_u, peer_v, norm_final):
    bsz, seq, d = x.shape
    t = bsz * seq
    depth = w_in.shape[0]
    assert depth == 1, "the final rmsnorm is fused into the single layer's peer call"
    xt = x.reshape(t, d)
    tm_proj = min(512, t)
    tm_mix = min(256, seq)
    tm_ret = min(256, t)
    tt = 128
    for l in range(depth):
        proj = _in_proj(xt, norm_mix[l][None, :], w_in[l].astype(_BF16), tm_proj, 1024)
        x1, h2 = _mixer(proj, xt, conv_w[l], conv_b[l][None, :], pool_w[l].astype(_BF16),
                        pool_scale[l][None, :], w_out[l].astype(_BF16), norm_ffn[l][None, :],
                        seq, tm_mix)
        sk = peer_sub_keys[l].reshape(2 * PEER_HEADS, PEER_N_KEYS, PEER_HALF).astype(_BF16)
        eidx_t, gates_t = _retrieval(h2, peer_w_q[l].astype(_BF16), sk, tm_ret)
        uv = _pack_experts(peer_u[l], peer_v[l], 256)
        xt = _peer(eidx_t.T.reshape(-1), h2, x1, gates_t, norm_final[None, :], uv, tt)
    return xt.reshape(bsz, seq, d)
```

```python
import functools

import jax
import jax.numpy as jnp
from jax import lax
from jax.experimental import pallas as pl
from jax.experimental.pallas import tpu as pltpu

CONV_K = 3
POOL_WINDOWS = (2, 4, 8, 16)
PEER_HEADS = 8
PEER_N_KEYS = 128
PEER_TOPK = 16
PEER_HALF = 128
RMS_EPS = 1e-6

HALO = 16
LANES = 128
NEG_INF = float("-inf")

_BF16 = jnp.bfloat16
_F32 = jnp.float32
_PACKED = jnp.uint32


def _rmsnorm(xf, g):
    ms = jnp.mean(xf * xf, axis=-1, keepdims=True)
    return xf * lax.rsqrt(ms + RMS_EPS) * g


def _gelu_tanh(x):
    return 0.5 * x * (1.0 + jnp.tanh(0.7978845608028654 * (x + 0.044715 * (x * x * x))))


def _in_proj_body(x_ref, g_ref, w_ref, o_ref, h_scr):
    @pl.when(pl.program_id(1) == 0)
    def _():
        h_scr[...] = _rmsnorm(x_ref[...], g_ref[...]).astype(_BF16)

    o_ref[...] = jnp.dot(h_scr[...], w_ref[...], preferred_element_type=_F32)


def _in_proj(x2, g, w_bf16, tm, tn):
    t, d = x2.shape
    n = w_bf16.shape[1]
    return pl.pallas_call(
        _in_proj_body,
        grid=(t // tm, n // tn),
        in_specs=[
            pl.BlockSpec((tm, d), lambda i, j: (i, 0)),
            pl.BlockSpec((1, d), lambda i, j: (0, 0)),
            pl.BlockSpec((d, tn), lambda i, j: (0, j)),
        ],
        out_specs=pl.BlockSpec((tm, tn), lambda i, j: (i, j)),
        out_shape=jax.ShapeDtypeStruct((t, n), _F32),
        scratch_shapes=[pltpu.VMEM((tm, d), _BF16)],
        compiler_params=pltpu.CompilerParams(
            dimension_semantics=("arbitrary", "arbitrary"),
            vmem_limit_bytes=40 * 1024 * 1024),
        name="in_proj",
    )(x2, g, w_bf16)


def _mixer_body(proj_ref, halo_ref, x_ref, cw_ref, cb_ref, pw_ref, psc_ref, wo_ref, g2_ref,
                x1_ref, h2_ref, zs, ps, ycat, *, tm, tiles_per_seq, cw, gd):
    tile_in_seq = pl.program_id(0) % tiles_per_seq
    first = tile_in_seq == 0

    zh = halo_ref[:, cw:2 * cw] * halo_ref[:, 2 * cw:3 * cw]
    zs[0:HALO, :] = jnp.where(first, 0.0, zh)
    zs[HALO:, :] = proj_ref[:, cw:2 * cw] * proj_ref[:, 2 * cw:3 * cw]
    ps[0:HALO, :] = jnp.where(first, 0.0, halo_ref[:, 3 * cw:])
    ps[HALO:, :] = proj_ref[:, 3 * cw:]

    y = cb_ref[...]
    for k in range(CONV_K):
        off = HALO - (CONV_K - 1) + k
        y = y + cw_ref[k:k + 1, :] * zs[off:off + tm, :]
    ycat[:, 0:cw] = (proj_ref[:, 0:cw] * y).astype(_BF16)

    pos = tile_in_seq * tm + lax.broadcasted_iota(jnp.int32, (tm, 1), 0)
    for g, w in enumerate(POOL_WINDOWS):
        c0 = g * gd
        cur = ps[HALO:HALO + tm, c0:c0 + gd]
        acc = cur
        for k in range(1, w):
            acc = acc + ps[HALO - k:HALO - k + tm, c0:c0 + gd]
        count = jnp.minimum(pos + 1, w).astype(_F32)
        pooled = acc / count - cur
        yp = jnp.dot(pooled.astype(_BF16), pw_ref[g], preferred_element_type=_F32)
        ycat[:, cw + c0:cw + c0 + gd] = (yp * psc_ref[:, c0:c0 + gd]).astype(_BF16)

    x1 = x_ref[...] + jnp.dot(ycat[...], wo_ref[...], preferred_element_type=_F32)
    x1_ref[...] = x1
    h2_ref[...] = _rmsnorm(x1, g2_ref[...])


def _mixer(proj, x2, conv_w, conv_b, pool_w_bf16, pool_scale, w_out_bf16, g2, seq, tm):
    t, d = x2.shape
    n = proj.shape[1]
    cw = conv_w.shape[1]
    gd = pool_w_bf16.shape[1]
    hb = tm // HALO
    body = functools.partial(_mixer_body, tm=tm, tiles_per_seq=seq // tm, cw=cw, gd=gd)
    return pl.pallas_call(
        body,
        grid=(t // tm,),
        in_specs=[
            pl.BlockSpec((tm, n), lambda i: (i, 0)),
            pl.BlockSpec((HALO, n), lambda i: (jnp.maximum(i * hb - 1, 0), 0)),
            pl.BlockSpec((tm, d), lambda i: (i, 0)),
            pl.BlockSpec((CONV_K, cw), lambda i: (0, 0)),
            pl.BlockSpec((1, cw), lambda i: (0, 0)),
            pl.BlockSpec((len(POOL_WINDOWS), gd, gd), lambda i: (0, 0, 0)),
            pl.BlockSpec((1, cw), lambda i: (0, 0)),
            pl.BlockSpec((d, d), lambda i: (0, 0)),
            pl.BlockSpec((1, d), lambda i: (0, 0)),
        ],
        out_specs=[pl.BlockSpec((tm, d), lambda i: (i, 0)),
                   pl.BlockSpec((tm, d), lambda i: (i, 0))],
        out_shape=[jax.ShapeDtypeStruct((t, d), _F32), jax.ShapeDtypeStruct((t, d), _F32)],
        scratch_shapes=[pltpu.VMEM((HALO + tm, cw), _F32),
                        pltpu.VMEM((HALO + tm, cw), _F32),
                        pltpu.VMEM((tm, d), _BF16)],
        compiler_params=pltpu.CompilerParams(
            dimension_semantics=("arbitrary",),
            vmem_limit_bytes=52 * 1024 * 1024),
        name="mixer",
    )(proj, proj, x2, conv_w, conv_b, pool_w_bf16, pool_scale, w_out_bf16, g2)


def _topk_rows(s, k):
    n = s.shape[0]
    iota = lax.broadcasted_iota(jnp.int32, s.shape, 0)
    vals, idxs = [], []
    for _ in range(k):
        m = jnp.max(s, axis=0, keepdims=True)
        i = jnp.min(jnp.where(s == m, iota, n), axis=0, keepdims=True)
        vals.append(m)
        idxs.append(i)
        s = jnp.where(iota == i, NEG_INF, s)
    return jnp.concatenate(vals, axis=0), jnp.concatenate(idxs, axis=0)


def _select_rows(table, sel):
    out = jnp.zeros(sel.shape, table.dtype)
    for i in range(table.shape[0]):
        out = jnp.where(sel == i, table[i:i + 1, :], out)
    return out


_PAIRS = [(i, j) for i in range(PEER_TOPK) for j in range(PEER_TOPK) if (i + 1) * (j + 1) <= PEER_TOPK]
_PAIR_ROWS = -(-len(_PAIRS) // 8) * 8


def _retrieval_body(h_ref, wq_ref, sk_ref, eidx_ref, gate_ref):
    q = jnp.dot(h_ref[...].astype(_BF16), wq_ref[...], preferred_element_type=_F32)
    nt = (((1,), (1,)), ((), ()))
    tm = q.shape[0]
    pad = _PAIR_ROWS - len(_PAIRS)
    for h in range(PEER_HEADS):
        sv, si = [], []
        for p in range(2):
            hp = 2 * h + p
            qhp = q[:, hp * PEER_HALF:(hp + 1) * PEER_HALF].astype(_BF16)
            s = lax.dot_general(sk_ref[hp], qhp, nt, preferred_element_type=_F32)
            v, i = _topk_rows(s, PEER_TOPK)
            sv.append(v)
            si.append(i)
        cand = jnp.concatenate([sv[0][i:i + 1, :] + sv[1][j:j + 1, :] for i, j in _PAIRS]
                               + [jnp.full((pad, tm), NEG_INF, _F32)], axis=0)
        cidx = jnp.concatenate([si[0][i:i + 1, :] * PEER_N_KEYS + si[1][j:j + 1, :] for i, j in _PAIRS]
                               + [jnp.zeros((pad, tm), jnp.int32)], axis=0)
        top_s, top_row = _topk_rows(cand, PEER_TOPK)
        ex = jnp.exp(top_s - jnp.max(top_s, axis=0, keepdims=True))
        r0 = h * PEER_TOPK
        eidx_ref[r0:r0 + PEER_TOPK, :] = _select_rows(cidx, top_row)
        gate_ref[r0:r0 + PEER_TOPK, :] = ex / jnp.sum(ex, axis=0, keepdims=True)


def _retrieval(h2, wq_bf16, sk_bf16, tm):
    t, d = h2.shape
    hk = PEER_HEADS * PEER_TOPK
    return pl.pallas_call(
        _retrieval_body,
        grid=(t // tm,),
        in_specs=[
            pl.BlockSpec((tm, d), lambda i: (i, 0)),
            pl.BlockSpec(wq_bf16.shape, lambda i: (0, 0)),
            pl.BlockSpec(sk_bf16.shape, lambda i: (0, 0, 0)),
        ],
        out_specs=[pl.BlockSpec((hk, tm), lambda i: (0, i)),
                   pl.BlockSpec((hk, tm), lambda i: (0, i))],
        out_shape=[jax.ShapeDtypeStruct((hk, t), jnp.int32), jax.ShapeDtypeStruct((hk, t), _F32)],
        compiler_params=pltpu.CompilerParams(
            dimension_semantics=("arbitrary",),
            vmem_limit_bytes=40 * 1024 * 1024),
        name="retrieval",
    )(h2, wq_bf16, sk_bf16)


def _pack_body(u_ref, v_ref, o_ref):
    o_ref[...] = pltpu.pack_elementwise([u_ref[...], v_ref[...]], packed_dtype=_BF16)


def _unpack(w, half):
    return pltpu.unpack_elementwise(w, index=half, packed_dtype=_BF16, unpacked_dtype=_F32)


def _pack_experts(u, v, rows):
    n, d = u.shape
    return pl.pallas_call(
        _pack_body,
        grid=(n // rows,),
        in_specs=[pl.BlockSpec((rows, d), lambda i: (i, 0)),
                  pl.BlockSpec((rows, d), lambda i: (i, 0))],
        out_specs=pl.BlockSpec((rows, d), lambda i: (i, 0)),
        out_shape=jax.ShapeDtypeStruct((n, d), _PACKED),
        compiler_params=pltpu.CompilerParams(dimension_semantics=("arbitrary",)),
        name="pack_experts",
    )(u, v)


PEER_SLOTS = 4


def _peer_body(idx_ref, h_ref, x1_ref, gate_ref, gf_ref, uv_hbm, uv_slabs, o_ref, *scratch, tt, hk, d):
    bufs, (y_scr, sem) = scratch[:PEER_SLOTS], scratch[PEER_SLOTS:]
    ahead = PEER_SLOTS - 1
    chunks = d // LANES

    def issue(t, slot):
        for k in range(hk):
            e = idx_ref[t * hk + k]
            pltpu.make_async_copy(uv_hbm.at[e], bufs[slot].at[:, k, :],
                                  sem.at[slot]).start(priority=k % 2)

    def wait_all(slot):
        pltpu.make_async_copy(uv_slabs.at[pl.ds(0, chunks)], bufs[slot], sem.at[slot]).wait()

    lane_tok = lax.broadcasted_iota(jnp.int32, (hk, tt), 1)

    def compute(t, slot):
        x = h_ref[pl.ds(t, 1), :]
        acc = jnp.zeros((hk, LANES), _F32)
        for c in range(chunks):
            acc = acc + _unpack(bufs[slot][c], 0) * x[:, c * LANES:(c + 1) * LANES]
        a = jnp.sum(acc, axis=1, keepdims=True)
        g = jnp.sum(jnp.where(lane_tok == t, gate_ref[...], 0.0), axis=1, keepdims=True)
        coef = g * _gelu_tanh(a)
        y_scr[pl.ds(t, 1), :] = jnp.concatenate(
            [jnp.sum(coef * _unpack(bufs[slot][c], 1), axis=0, keepdims=True) for c in range(chunks)],
            axis=1)

    def group(t0, n_issue):
        for s in range(PEER_SLOTS):
            wait_all(s)
            if s < n_issue:
                issue(t0 + s + ahead, (s + ahead) % PEER_SLOTS)
            compute(t0 + s, s)

    for s in range(ahead):
        issue(s, s)

    def steady(j, carry):
        group(PEER_SLOTS * j, PEER_SLOTS)
        return carry

    lax.fori_loop(0, tt // PEER_SLOTS - 1, steady, 0)
    group(tt - PEER_SLOTS, PEER_SLOTS - ahead)
    o_ref[...] = _rmsnorm(x1_ref[...] + y_scr[...], gf_ref[...])


def _peer(eidx_flat, h2, x1, gates_t, g_final, uv, tt):
    t, d = h2.shape
    hk = gates_t.shape[0]
    body = functools.partial(_peer_body, tt=tt, hk=hk, d=d)
    return pl.pallas_call(
        body,
        grid=(t // tt,),
        in_specs=[
            pl.BlockSpec((tt * hk,), lambda i: (i,), memory_space=pltpu.SMEM),
            pl.BlockSpec((tt, d), lambda i: (i, 0)),
            pl.BlockSpec((tt, d), lambda i: (i, 0)),
            pl.BlockSpec((hk, tt), lambda i: (0, i)),
            pl.BlockSpec((1, d), lambda i: (0, 0)),
            pl.BlockSpec(memory_space=pl.ANY),
            pl.BlockSpec(memory_space=pl.ANY),
        ],
        out_specs=pl.BlockSpec((tt, d), lambda i: (i, 0)),
        out_shape=jax.ShapeDtypeStruct((t, d), _F32),
        scratch_shapes=[pltpu.VMEM((d // LANES, hk, LANES), _PACKED)] * PEER_SLOTS
                       + [pltpu.VMEM((tt, d), _F32), pltpu.SemaphoreType.DMA((PEER_SLOTS,))],
        compiler_params=pltpu.CompilerParams(
            dimension_semantics=("arbitrary",),
            vmem_limit_bytes=40 * 1024 * 1024),
        name="peer",
    )(eidx_flat, h2, x1, gates_t, g_final,
      uv.reshape(-1, d // LANES, LANES), uv.reshape(-1, hk, LANES))


def kernel(x, norm_mix, w_in, conv_w, conv_b, pool_w, pool_scale, w_out, norm_ffn,
           peer_w_q, peer_sub_keys, peer_u, peer_v, norm_final):
    bsz, seq, d = x.shape
    t = bsz * seq
    depth = w_in.shape[0]
    assert depth == 1, "the final rmsnorm is fused into the single layer's peer call"
    xt = x.reshape(t, d)
    tm_proj = min(512, t)
    tm_mix = min(256, seq)
    tm_ret = min(256, t)
    tt = 128
    for l in range(depth):
        proj = _in_proj(xt, norm_mix[l][None, :], w_in[l].astype(_BF16), tm_proj, 1024)
        x1, h2 = _mixer(proj, xt, conv_w[l], conv_b[l][None, :], pool_w[l].astype(_BF16),
                        pool_scale[l][None, :], w_out[l].astype(_BF16), norm_ffn[l][None, :],
                        seq, tm_mix)
        sk = peer_sub_keys[l].reshape(2 * PEER_HEADS, PEER_N_KEYS, PEER_HALF).astype(_BF16)
        eidx_t, gates_t = _retrieval(h2, peer_w_q[l].astype(_BF16), sk, tm_ret)
        uv = _pack_experts(peer_u[l], peer_v[l], 256)
        xt = _peer(eidx_t.T.reshape(-1), h2, x1, gates_t, norm_final[None, :], uv, tt)
    return xt.reshape(bsz, seq, d)
```

```python
import functools

import jax
import jax.numpy as jnp
from jax import lax
from jax.experimental import pallas as pl
from jax.experimental.pallas import tpu as pltpu

CONV_K = 3
POOL_WINDOWS = (2, 4, 8, 16)
PEER_HEADS = 8
PEER_N_KEYS = 128
PEER_TOPK = 16
PEER_HALF = 128
RMS_EPS = 1e-6

HALO = 16
LANES = 128
NEG_INF = float("-inf")

_BF16 = jnp.bfloat16
_F32 = jnp.float32
_PACKED = jnp.uint32


def _rmsnorm(xf, g):
    ms = jnp.mean(xf * xf, axis=-1, keepdims=True)
    return xf * lax.rsqrt(ms + RMS_EPS) * g


def _gelu_tanh(x):
    return 0.5 * x * (1.0 + jnp.tanh(0.7978845608028654 * (x + 0.044715 * (x * x * x))))


def _in_proj_body(x_ref, g_ref, w_ref, o_ref, h_scr):
    @pl.when(pl.program_id(1) == 0)
    def _():
        h_scr[...] = _rmsnorm(x_ref[...], g_ref[...]).astype(_BF16)

    o_ref[...] = jnp.dot(h_scr[...], w_ref[...], preferred_element_type=_F32)


def _in_proj(x2, g, w_bf16, tm, tn):
    t, d = x2.shape
    n = w_bf16.shape[1]
    return pl.pallas_call(
        _in_proj_body,
        grid=(t // tm, n // tn),
        in_specs=[
            pl.BlockSpec((tm, d), lambda i, j: (i, 0)),
            pl.BlockSpec((1, d), lambda i, j: (0, 0)),
            pl.BlockSpec((d, tn), lambda i, j: (0, j)),
        ],
        out_specs=pl.BlockSpec((tm, tn), lambda i, j: (i, j)),
        out_shape=jax.ShapeDtypeStruct((t, n), _F32),
        scratch_shapes=[pltpu.VMEM((tm, d), _BF16)],
        compiler_params=pltpu.CompilerParams(
            dimension_semantics=("arbitrary", "arbitrary"),
            vmem_limit_bytes=40 * 1024 * 1024),
        name="in_proj",
    )(x2, g, w_bf16)


def _mixer_body(proj_ref, halo_ref, x_ref, cw_ref, cb_ref, pw_ref, psc_ref, wo_ref, g2_ref,
                x1_ref, h2_ref, zs, ps, ycat, *, tm, tiles_per_seq, cw, gd):
    tile_in_seq = pl.program_id(0) % tiles_per_seq
    first = tile_in_seq == 0

    zh = halo_ref[:, cw:2 * cw] * halo_ref[:, 2 * cw:3 * cw]
    zs[0:HALO, :] = jnp.where(first, 0.0, zh)
    zs[HALO:, :] = proj_ref[:, cw:2 * cw] * proj_ref[:, 2 * cw:3 * cw]
    ps[0:HALO, :] = jnp.where(first, 0.0, halo_ref[:, 3 * cw:])
    ps[HALO:, :] = proj_ref[:, 3 * cw:]

    y = cb_ref[...]
    for k in range(CONV_K):
        off = HALO - (CONV_K - 1) + k
        y = y + cw_ref[k:k + 1, :] * zs[off:off + tm, :]
    ycat[:, 0:cw] = (proj_ref[:, 0:cw] * y).astype(_BF16)

    pos = tile_in_seq * tm + lax.broadcasted_iota(jnp.int32, (tm, 1), 0)
    for g, w in enumerate(POOL_WINDOWS):
        c0 = g * gd
        cur = ps[HALO:HALO + tm, c0:c0 + gd]
        acc = cur
        for k in range(1, w):
            acc = acc + ps[HALO - k:HALO - k + tm, c0:c0 + gd]
        count = jnp.minimum(pos + 1, w).astype(_F32)
        pooled = acc / count - cur
        yp = jnp.dot(pooled.astype(_BF16), pw_ref[g], preferred_element_type=_F32)
        ycat[:, cw + c0:cw + c0 + gd] = (yp * psc_ref[:, c0:c0 + gd]).astype(_BF16)

    x1 = x_ref[...] + jnp.dot(ycat[...], wo_ref[...], preferred_element_type=_F32)
    x1_ref[...] = x1
    h2_ref[...] = _rmsnorm(x1, g2_ref[...])


def _mixer(proj, x2, conv_w, conv_b, pool_w_bf16, pool_scale, w_out_bf16, g2, seq, tm):
    t, d = x2.shape
    n = proj.shape[1]
    cw = conv_w.shape[1]
    gd = pool_w_bf16.shape[1]
    hb = tm // HALO
    body = functools.partial(_mixer_body, tm=tm, tiles_per_seq=seq // tm, cw=cw, gd=gd)
    return pl.pallas_call(
        body,
        grid=(t // tm,),
        in_specs=[
            pl.BlockSpec((tm, n), lambda i: (i, 0)),
            pl.BlockSpec((HALO, n), lambda i: (jnp.maximum(i * hb - 1, 0), 0)),
            pl.BlockSpec((tm, d), lambda i: (i, 0)),
            pl.BlockSpec((CONV_K, cw), lambda i: (0, 0)),
            pl.BlockSpec((1, cw), lambda i: (0, 0)),
            pl.BlockSpec((len(POOL_WINDOWS), gd, gd), lambda i: (0, 0, 0)),
            pl.BlockSpec((1, cw), lambda i: (0, 0)),
            pl.BlockSpec((d, d), lambda i: (0, 0)),
            pl.BlockSpec((1, d), lambda i: (0, 0)),
        ],
        out_specs=[pl.BlockSpec((tm, d), lambda i: (i, 0)),
                   pl.BlockSpec((tm, d), lambda i: (i, 0))],
        out_shape=[jax.ShapeDtypeStruct((t, d), _F32), jax.ShapeDtypeStruct((t, d), _F32)],
        scratch_shapes=[pltpu.VMEM((HALO + tm, cw), _F32),
                        pltpu.VMEM((HALO + tm, cw), _F32),
                        pltpu.VMEM((tm, d), _BF16)],
        compiler_params=pltpu.CompilerParams(
            dimension_semantics=("arbitrary",),
            vmem_limit_bytes=52 * 1024 * 1024),
        name="mixer",
    )(proj, proj, x2, conv_w, conv_b, pool_w_bf16, pool_scale, w_out_bf16, g2)


def _topk_rows(s, k):
    n = s.shape[0]
    iota = lax.broadcasted_iota(jnp.int32, s.shape, 0)
    vals, idxs = [], []
    for _ in range(k):
        m = jnp.max(s, axis=0, keepdims=True)
        i = jnp.min(jnp.where(s == m, iota, n), axis=0, keepdims=True)
        vals.append(m)
        idxs.append(i)
        s = jnp.where(iota == i, NEG_INF, s)
    return jnp.concatenate(vals, axis=0), jnp.concatenate(idxs, axis=0)


def _select_rows(table, sel):
    out = jnp.zeros(sel.shape, table.dtype)
    for i in range(table.shape[0]):
        out = jnp.where(sel == i, table[i:i + 1, :], out)
    return out


_PAIRS = [(i, j) for i in range(PEER_TOPK) for j in range(PEER_TOPK) if (i + 1) * (j + 1) <= PEER_TOPK]
_PAIR_ROWS = -(-len(_PAIRS) // 8) * 8


def _retrieval_body(h_ref, wq_ref, sk_ref, eidx_ref, gate_ref):
    q = jnp.dot(h_ref[...].astype(_BF16), wq_ref[...], preferred_element_type=_F32)
    nt = (((1,), (1,)), ((), ()))
    tm = q.shape[0]
    pad = _PAIR_ROWS - len(_PAIRS)
    for h in range(PEER_HEADS):
        sv, si = [], []
        for p in range(2):
            hp = 2 * h + p
            qhp = q[:, hp * PEER_HALF:(hp + 1) * PEER_HALF].astype(_BF16)
            s = lax.dot_general(sk_ref[hp], qhp, nt, preferred_element_type=_F32)
            v, i = _topk_rows(s, PEER_TOPK)
            sv.append(v)
            si.append(i)
        cand = jnp.concatenate([sv[0][i:i + 1, :] + sv[1][j:j + 1, :] for i, j in _PAIRS]
                               + [jnp.full((pad, tm), NEG_INF, _F32)], axis=0)
        cidx = jnp.concatenate([si[0][i:i + 1, :] * PEER_N_KEYS + si[1][j:j + 1, :] for i, j in _PAIRS]
                               + [jnp.zeros((pad, tm), jnp.int32)], axis=0)
        top_s, top_row = _topk_rows(cand, PEER_TOPK)
        ex = jnp.exp(top_s - jnp.max(top_s, axis=0, keepdims=True))
        r0 = h * PEER_TOPK
        eidx_ref[r0:r0 + PEER_TOPK, :] = _select_rows(cidx, top_row)
        gate_ref[r0:r0 + PEER_TOPK, :] = ex / jnp.sum(ex, axis=0, keepdims=True)


def _retrieval(h2, wq_bf16, sk_bf16, tm):
    t, d = h2.shape
    hk = PEER_HEADS * PEER_TOPK
    return pl.pallas_call(
        _retrieval_body,
        grid=(t // tm,),
        in_specs=[
            pl.BlockSpec((tm, d), lambda i: (i, 0)),
            pl.BlockSpec(wq_bf16.shape, lambda i: (0, 0)),
            pl.BlockSpec(sk_bf16.shape, lambda i: (0, 0, 0)),
        ],
        out_specs=[pl.BlockSpec((hk, tm), lambda i: (0, i)),
                   pl.BlockSpec((hk, tm), lambda i: (0, i))],
        out_shape=[jax.ShapeDtypeStruct((hk, t), jnp.int32), jax.ShapeDtypeStruct((hk, t), _F32)],
        compiler_params=pltpu.CompilerParams(
            dimension_semantics=("arbitrary",),
            vmem_limit_bytes=40 * 1024 * 1024),
        name="retrieval",
    )(h2, wq_bf16, sk_bf16)


def _pack_body(u_ref, v_ref, o_ref):
    w = pltpu.pack_elementwise([u_ref[...], v_ref[...]], packed_dtype=_BF16)
    for c in range(o_ref.shape[1]):
        o_ref[:, c, :] = w[:, c * LANES:(c + 1) * LANES]


def _unpack(w, half):
    return pltpu.unpack_elementwise(w, index=half, packed_dtype=_BF16, unpacked_dtype=_F32)


def _pack_experts(u, v, rows):
    n, d = u.shape
    return pl.pallas_call(
        _pack_body,
        grid=(n // rows,),
        in_specs=[pl.BlockSpec((rows, d), lambda i: (i, 0)),
                  pl.BlockSpec((rows, d), lambda i: (i, 0))],
        out_specs=pl.BlockSpec((rows, d // LANES, LANES), lambda i: (i, 0, 0)),
        out_shape=jax.ShapeDtypeStruct((n, d // LANES, LANES), _PACKED),
        compiler_params=pltpu.CompilerParams(dimension_semantics=("arbitrary",)),
        name="pack_experts",
    )(u, v)


PEER_SLOTS = 8


def _peer_body(idx_ref, nxt_ref, h_ref, x1_ref, gate_ref, gf_ref, uv_hbm, uv_slabs, o_ref, *scratch,
               tt, hk, d):
    bufs, (y_scr, sem) = scratch[:PEER_SLOTS], scratch[PEER_SLOTS:]
    ahead = PEER_SLOTS - 1
    chunks = d // LANES
    step = pl.program_id(0)

    def issue(ids_ref, t, slot):
        for k in range(hk):
            e = ids_ref[t * hk + k]
            pltpu.make_async_copy(uv_hbm.at[e], bufs[slot].at[:, k, :],
                                  sem.at[slot]).start(priority=k % 2)

    def wait_all(slot):
        pltpu.make_async_copy(uv_slabs.at[pl.ds(0, chunks)], bufs[slot], sem.at[slot]).wait()

    lane_tok = lax.broadcasted_iota(jnp.int32, (hk, tt), 1)

    def compute(t, slot):
        x = h_ref[pl.ds(t, 1), :]
        acc = jnp.zeros((hk, LANES), _F32)
        for c in range(chunks):
            acc = acc + _unpack(bufs[slot][c], 0) * x[:, c * LANES:(c + 1) * LANES]
        a = jnp.sum(acc, axis=1, keepdims=True)
        g = jnp.sum(jnp.where(lane_tok == t, gate_ref[...], 0.0), axis=1, keepdims=True)
        coef = g * _gelu_tanh(a)
        y_scr[pl.ds(t, 1), :] = jnp.concatenate(
            [jnp.sum(coef * _unpack(bufs[slot][c], 1), axis=0, keepdims=True) for c in range(chunks)],
            axis=1)

    @pl.when(step == 0)
    def _():
        for s in range(ahead):
            issue(idx_ref, s, s)

    def steady(j, carry):
        t0 = PEER_SLOTS * j
        for s in range(PEER_SLOTS):
            wait_all(s)
            issue(idx_ref, t0 + s + ahead, (s + ahead) % PEER_SLOTS)
            compute(t0 + s, s)
        return carry

    lax.fori_loop(0, tt // PEER_SLOTS - 1, steady, 0)

    t0 = tt - PEER_SLOTS
    for s in range(PEER_SLOTS):
        wait_all(s)
        if s == 0:
            issue(idx_ref, tt - 1, ahead)
        else:
            @pl.when(step + 1 < pl.num_programs(0))
            def _():
                issue(nxt_ref, s - 1, s - 1)
        compute(t0 + s, s)

    o_ref[...] = _rmsnorm(x1_ref[...] + y_scr[...], gf_ref[...])


def _peer(eidx_flat, h2, x1, gates_t, g_final, uv, tt):
    t, d = h2.shape
    hk = gates_t.shape[0]
    body = functools.partial(_peer_body, tt=tt, hk=hk, d=d)
    last = t // tt - 1
    return pl.pallas_call(
        body,
        grid=(t // tt,),
        in_specs=[
            pl.BlockSpec((tt * hk,), lambda i: (i,), memory_space=pltpu.SMEM),
            pl.BlockSpec((tt * hk,), lambda i: (jnp.minimum(i + 1, last),), memory_space=pltpu.SMEM),
            pl.BlockSpec((tt, d), lambda i: (i, 0)),
            pl.BlockSpec((tt, d), lambda i: (i, 0)),
            pl.BlockSpec((hk, tt), lambda i: (0, i)),
            pl.BlockSpec((1, d), lambda i: (0, 0)),
            pl.BlockSpec(memory_space=pl.ANY),
            pl.BlockSpec(memory_space=pl.ANY),
        ],
        out_specs=pl.BlockSpec((tt, d), lambda i: (i, 0)),
        out_shape=jax.ShapeDtypeStruct((t, d), _F32),
        scratch_shapes=[pltpu.VMEM((d // LANES, hk, LANES), _PACKED)] * PEER_SLOTS
                       + [pltpu.VMEM((tt, d), _F32), pltpu.SemaphoreType.DMA((PEER_SLOTS,))],
        compiler_params=pltpu.CompilerParams(
            dimension_semantics=("arbitrary",),
            vmem_limit_bytes=40 * 1024 * 1024),
        name="peer",
    )(eidx_flat, eidx_flat, h2, x1, gates_t, g_final, uv, uv.reshape(-1, hk, LANES))


def kernel(x, norm_mix, w_in, conv_w, conv_b, pool_w, pool_scale, w_out, norm_ffn,
           peer_w_q, peer_sub_keys, peer_u, peer_v, norm_final):
    bsz, seq, d = x.shape
    t = bsz * seq
    depth = w_in.shape[0]
    assert depth == 1, "the final rmsnorm is fused into the single layer's peer call"
    xt = x.reshape(t, d)
    tm_proj = min(512, t)
    tm_mix = min(256, seq)
    tm_ret = min(256, t)
    tt = 128
    for l in range(depth):
        proj = _in_proj(xt, norm_mix[l][None, :], w_in[l].astype(_BF16), tm_proj, 1024)
        x1, h2 = _mixer(proj, xt, conv_w[l], conv_b[l][None, :], pool_w[l].astype(_BF16),
                        pool_scale[l][None, :], w_out[l].astype(_BF16), norm_ffn[l][None, :],
                        seq, tm_mix)
        sk = peer_sub_keys[l].reshape(2 * PEER_HEADS, PEER_N_KEYS, PEER_HALF).astype(_BF16)
        eidx_t, gates_t = _retrieval(h2, peer_w_q[l].astype(_BF16), sk, tm_ret)
        uv = _pack_experts(peer_u[l], peer_v[l], 256)
        xt = _peer(eidx_t.T.reshape(-1), h2, x1, gates_t, norm_final[None, :], uv, tt)
    return xt.reshape(bsz, seq, d)
```

```python
import functools

import jax
import jax.numpy as jnp
from jax import lax
from jax.experimental import pallas as pl
from jax.experimental.pallas import tpu as pltpu

CONV_K = 3
POOL_WINDOWS = (2, 4, 8, 16)
PEER_HEADS = 8
PEER_N_KEYS = 128
PEER_TOPK = 16
PEER_HALF = 128
RMS_EPS = 1e-6

HALO = 16
LANES = 128
NEG_INF = float("-inf")

_BF16 = jnp.bfloat16
_F32 = jnp.float32
_PACKED = jnp.uint32


def _rmsnorm(xf, g):
    ms = jnp.mean(xf * xf, axis=-1, keepdims=True)
    return xf * lax.rsqrt(ms + RMS_EPS) * g


def _gelu_tanh(x):
    return 0.5 * x * (1.0 + jnp.tanh(0.7978845608028654 * (x + 0.044715 * (x * x * x))))


def _in_proj_body(x_ref, g_ref, w_ref, o_ref, h_scr):
    @pl.when(pl.program_id(1) == 0)
    def _():
        h_scr[...] = _rmsnorm(x_ref[...], g_ref[...]).astype(_BF16)

    o_ref[...] = jnp.dot(h_scr[...], w_ref[...], preferred_element_type=_F32)


def _in_proj(x2, g, w_bf16, tm, tn):
    t, d = x2.shape
    n = w_bf16.shape[1]
    return pl.pallas_call(
        _in_proj_body,
        grid=(t // tm, n // tn),
        in_specs=[
            pl.BlockSpec((tm, d), lambda i, j: (i, 0)),
            pl.BlockSpec((1, d), lambda i, j: (0, 0)),
            pl.BlockSpec((d, tn), lambda i, j: (0, j)),
        ],
        out_specs=pl.BlockSpec((tm, tn), lambda i, j: (i, j)),
        out_shape=jax.ShapeDtypeStruct((t, n), _F32),
        scratch_shapes=[pltpu.VMEM((tm, d), _BF16)],
        compiler_params=pltpu.CompilerParams(
            dimension_semantics=("arbitrary", "arbitrary"),
            vmem_limit_bytes=40 * 1024 * 1024),
        name="in_proj",
    )(x2, g, w_bf16)


def _mixer_body(proj_ref, halo_ref, x_ref, cw_ref, cb_ref, pw_ref, psc_ref, wo_ref, g2_ref,
                x1_ref, h2_ref, zs, ps, ycat, *, tm, tiles_per_seq, cw, gd):
    tile_in_seq = pl.program_id(0) % tiles_per_seq
    first = tile_in_seq == 0

    zh = halo_ref[:, cw:2 * cw] * halo_ref[:, 2 * cw:3 * cw]
    zs[0:HALO, :] = jnp.where(first, 0.0, zh)
    zs[HALO:, :] = proj_ref[:, cw:2 * cw] * proj_ref[:, 2 * cw:3 * cw]
    ps[0:HALO, :] = jnp.where(first, 0.0, halo_ref[:, 3 * cw:])
    ps[HALO:, :] = proj_ref[:, 3 * cw:]

    y = cb_ref[...]
    for k in range(CONV_K):
        off = HALO - (CONV_K - 1) + k
        y = y + cw_ref[k:k + 1, :] * zs[off:off + tm, :]
    ycat[:, 0:cw] = (proj_ref[:, 0:cw] * y).astype(_BF16)

    pos = tile_in_seq * tm + lax.broadcasted_iota(jnp.int32, (tm, 1), 0)
    for g, w in enumerate(POOL_WINDOWS):
        c0 = g * gd
        cur = ps[HALO:HALO + tm, c0:c0 + gd]
        acc = cur
        for k in range(1, w):
            acc = acc + ps[HALO - k:HALO - k + tm, c0:c0 + gd]
        count = jnp.minimum(pos + 1, w).astype(_F32)
        pooled = acc / count - cur
        yp = jnp.dot(pooled.astype(_BF16), pw_ref[g], preferred_element_type=_F32)
        ycat[:, cw + c0:cw + c0 + gd] = (yp * psc_ref[:, c0:c0 + gd]).astype(_BF16)

    x1 = x_ref[...] + jnp.dot(ycat[...], wo_ref[...], preferred_element_type=_F32)
    x1_ref[...] = x1
    h2_ref[...] = _rmsnorm(x1, g2_ref[...])


def _mixer(proj, x2, conv_w, conv_b, pool_w_bf16, pool_scale, w_out_bf16, g2, seq, tm):
    t, d = x2.shape
    n = proj.shape[1]
    cw = conv_w.shape[1]
    gd = pool_w_bf16.shape[1]
    hb = tm // HALO
    body = functools.partial(_mixer_body, tm=tm, tiles_per_seq=seq // tm, cw=cw, gd=gd)
    return pl.pallas_call(
        body,
        grid=(t // tm,),
        in_specs=[
            pl.BlockSpec((tm, n), lambda i: (i, 0)),
            pl.BlockSpec((HALO, n), lambda i: (jnp.maximum(i * hb - 1, 0), 0)),
            pl.BlockSpec((tm, d), lambda i: (i, 0)),
            pl.BlockSpec((CONV_K, cw), lambda i: (0, 0)),
            pl.BlockSpec((1, cw), lambda i: (0, 0)),
            pl.BlockSpec((len(POOL_WINDOWS), gd, gd), lambda i: (0, 0, 0)),
            pl.BlockSpec((1, cw), lambda i: (0, 0)),
            pl.BlockSpec((d, d), lambda i: (0, 0)),
            pl.BlockSpec((1, d), lambda i: (0, 0)),
        ],
        out_specs=[pl.BlockSpec((tm, d), lambda i: (i, 0)),
                   pl.BlockSpec((tm, d), lambda i: (i, 0))],
        out_shape=[jax.ShapeDtypeStruct((t, d), _F32), jax.ShapeDtypeStruct((t, d), _F32)],
        scratch_shapes=[pltpu.VMEM((HALO + tm, cw), _F32),
                        pltpu.VMEM((HALO + tm, cw), _F32),
                        pltpu.VMEM((tm, d), _BF16)],
        compiler_params=pltpu.CompilerParams(
            dimension_semantics=("arbitrary",),
            vmem_limit_bytes=52 * 1024 * 1024),
        name="mixer",
    )(proj, proj, x2, conv_w, conv_b, pool_w_bf16, pool_scale, w_out_bf16, g2)


def _topk_rows(s, k):
    n = s.shape[0]
    iota = lax.broadcasted_iota(jnp.int32, s.shape, 0)
    vals, idxs = [], []
    for _ in range(k):
        m = jnp.max(s, axis=0, keepdims=True)
        i = jnp.min(jnp.where(s == m, iota, n), axis=0, keepdims=True)
        vals.append(m)
        idxs.append(i)
        s = jnp.where(iota == i, NEG_INF, s)
    return jnp.concatenate(vals, axis=0), jnp.concatenate(idxs, axis=0)


def _select_rows(table, sel):
    out = jnp.zeros(sel.shape, table.dtype)
    for i in range(table.shape[0]):
        out = jnp.where(sel == i, table[i:i + 1, :], out)
    return out


_PAIRS = [(i, j) for i in range(PEER_TOPK) for j in range(PEER_TOPK) if (i + 1) * (j + 1) <= PEER_TOPK]
_PAIR_ROWS = -(-len(_PAIRS) // 8) * 8


def _retrieval_body(h_ref, wq_ref, sk_ref, eidx_ref, gate_ref):
    q = jnp.dot(h_ref[...].astype(_BF16), wq_ref[...], preferred_element_type=_F32)
    nt = (((1,), (1,)), ((), ()))
    tm = q.shape[0]
    pad = _PAIR_ROWS - len(_PAIRS)
    for h in range(PEER_HEADS):
        sv, si = [], []
        for p in range(2):
            hp = 2 * h + p
            qhp = q[:, hp * PEER_HALF:(hp + 1) * PEER_HALF].astype(_BF16)
            s = lax.dot_general(sk_ref[hp], qhp, nt, preferred_element_type=_F32)
            v, i = _topk_rows(s, PEER_TOPK)
            sv.append(v)
            si.append(i)
        cand = jnp.concatenate([sv[0][i:i + 1, :] + sv[1][j:j + 1, :] for i, j in _PAIRS]
                               + [jnp.full((pad, tm), NEG_INF, _F32)], axis=0)
        cidx = jnp.concatenate([si[0][i:i + 1, :] * PEER_N_KEYS + si[1][j:j + 1, :] for i, j in _PAIRS]
                               + [jnp.zeros((pad, tm), jnp.int32)], axis=0)
        top_s, top_row = _topk_rows(cand, PEER_TOPK)
        ex = jnp.exp(top_s - jnp.max(top_s, axis=0, keepdims=True))
        r0 = h * PEER_TOPK
        eidx_ref[r0:r0 + PEER_TOPK, :] = _select_rows(cidx, top_row)
        gate_ref[r0:r0 + PEER_TOPK, :] = ex / jnp.sum(ex, axis=0, keepdims=True)


def _retrieval(h2, wq_bf16, sk_bf16, tm):
    t, d = h2.shape
    hk = PEER_HEADS * PEER_TOPK
    return pl.pallas_call(
        _retrieval_body,
        grid=(t // tm,),
        in_specs=[
            pl.BlockSpec((tm, d), lambda i: (i, 0)),
            pl.BlockSpec(wq_bf16.shape, lambda i: (0, 0)),
            pl.BlockSpec(sk_bf16.shape, lambda i: (0, 0, 0)),
        ],
        out_specs=[pl.BlockSpec((hk, tm), lambda i: (0, i)),
                   pl.BlockSpec((hk, tm), lambda i: (0, i))],
        out_shape=[jax.ShapeDtypeStruct((hk, t), jnp.int32), jax.ShapeDtypeStruct((hk, t), _F32)],
        compiler_params=pltpu.CompilerParams(
            dimension_semantics=("arbitrary",),
            vmem_limit_bytes=40 * 1024 * 1024),
        name="retrieval",
    )(h2, wq_bf16, sk_bf16)


def _pack_body(u_ref, v_ref, o_ref):
    w = pltpu.pack_elementwise([u_ref[...], v_ref[...]], packed_dtype=_BF16)
    for c in range(o_ref.shape[1]):
        o_ref[:, c, :] = w[:, c * LANES:(c + 1) * LANES]


def _unpack(w, half):
    return pltpu.unpack_elementwise(w, index=half, packed_dtype=_BF16, unpacked_dtype=_F32)


def _pack_experts(u, v, rows):
    n, d = u.shape
    return pl.pallas_call(
        _pack_body,
        grid=(n // rows,),
        in_specs=[pl.BlockSpec((rows, d), lambda i: (i, 0)),
                  pl.BlockSpec((rows, d), lambda i: (i, 0))],
        out_specs=pl.BlockSpec((rows, d // LANES, LANES), lambda i: (i, 0, 0)),
        out_shape=jax.ShapeDtypeStruct((n, d // LANES, LANES), _PACKED),
        compiler_params=pltpu.CompilerParams(dimension_semantics=("arbitrary",)),
        name="pack_experts",
    )(u, v)


PEER_SLOTS = 8


def _peer_body(idx_ref, nxt_ref, h_ref, x1_ref, gate_ref, gf_ref, uv_hbm, uv_slabs, o_ref, *scratch,
               tt, hk, d):
    bufs, (y_scr, sem) = scratch[:PEER_SLOTS], scratch[PEER_SLOTS:]
    ahead = PEER_SLOTS - 1
    chunks = d // LANES
    step = pl.program_id(0)

    def issue(ids_ref, t, slot):
        for k in range(hk):
            e = ids_ref[t * hk + k]
            pltpu.make_async_copy(uv_hbm.at[e], bufs[slot].at[:, k, :],
                                  sem.at[slot]).start(priority=k % 2)

    def wait_all(slot):
        pltpu.make_async_copy(uv_slabs.at[pl.ds(0, chunks)], bufs[slot], sem.at[slot]).wait()

    lane_tok = lax.broadcasted_iota(jnp.int32, (hk, tt), 1)

    def compute(t, slot):
        x = h_ref[pl.ds(t, 1), :]
        acc = jnp.zeros((hk, LANES), _F32)
        for c in range(chunks):
            acc = acc + _unpack(bufs[slot][c], 0) * x[:, c * LANES:(c + 1) * LANES]
        a = jnp.sum(acc, axis=1, keepdims=True)
        g = jnp.sum(jnp.where(lane_tok == t, gate_ref[...], 0.0), axis=1, keepdims=True)
        coef = g * _gelu_tanh(a)
        y_scr[pl.ds(t, 1), :] = jnp.concatenate(
            [jnp.sum(coef * _unpack(bufs[slot][c], 1), axis=0, keepdims=True) for c in range(chunks)],
            axis=1)

    @pl.when(step == 0)
    def _():
        for s in range(ahead):
            issue(idx_ref, s, s)

    def steady(j, carry):
        t0 = PEER_SLOTS * j
        for s in range(PEER_SLOTS):
            wait_all(s)
            issue(idx_ref, t0 + s + ahead, (s + ahead) % PEER_SLOTS)
            compute(t0 + s, s)
        return carry

    lax.fori_loop(0, tt // PEER_SLOTS - 1, steady, 0)

    t0 = tt - PEER_SLOTS
    for s in range(PEER_SLOTS):
        wait_all(s)
        if s == 0:
            issue(idx_ref, tt - 1, ahead)
        else:
            @pl.when(step + 1 < pl.num_programs(0))
            def _():
                issue(nxt_ref, s - 1, s - 1)
        compute(t0 + s, s)

    o_ref[...] = _rmsnorm(x1_ref[...] + y_scr[...], gf_ref[...])


def _peer(eidx_flat, h2, x1, gates_t, g_final, uv, tt):
    t, d = h2.shape
    hk = gates_t.shape[0]
    body = functools.partial(_peer_body, tt=tt, hk=hk, d=d)
    last = t // tt - 1
    return pl.pallas_call(
        body,
        grid=(t // tt,),
        in_specs=[
            pl.BlockSpec((tt * hk,), lambda i: (i,), memory_space=pltpu.SMEM),
            pl.BlockSpec((tt * hk,), lambda i: (jnp.minimum(i + 1, last),), memory_space=pltpu.SMEM),
            pl.BlockSpec((tt, d), lambda i: (i, 0)),
            pl.BlockSpec((tt, d), lambda i: (i, 0)),
            pl.BlockSpec((hk, tt), lambda i: (0, i)),
            pl.BlockSpec((1, d), lambda i: (0, 0)),
            pl.BlockSpec(memory_space=pl.ANY),
            pl.BlockSpec(memory_space=pl.ANY),
        ],
        out_specs=pl.BlockSpec((tt, d), lambda i: (i, 0)),
        out_shape=jax.ShapeDtypeStruct((t, d), _F32),
        scratch_shapes=[pltpu.VMEM((d // LANES, hk, LANES), _PACKED)] * PEER_SLOTS
                       + [pltpu.VMEM((tt, d), _F32), pltpu.SemaphoreType.DMA((PEER_SLOTS,))],
        compiler_params=pltpu.CompilerParams(
            dimension_semantics=("arbitrary",),
            vmem_limit_bytes=40 * 1024 * 1024),
        name="peer",
    )(eidx_flat, eidx_flat, h2, x1, gates_t, g_final, uv, uv.reshape(-1, hk, LANES))


def kernel(x, norm_mix, w_in, conv_w, conv_b, pool_w, pool_scale, w_out, norm_ffn,
           peer_w_q, peer_sub_keys, peer_u, peer_v, norm_final):
    bsz, seq, d = x.shape
    t = bsz * seq
    depth = w_in.shape[0]
    assert depth == 1, "the final rmsnorm is fused into the single layer's peer call"
    xt = x.reshape(t, d)
    tm_proj = min(512, t)
    tm_mix = min(256, seq)
    tm_ret = min(256, t)
    tt = 128
    for l in range(depth):
        proj = _in_proj(xt, norm_mix[l][None, :], w_in[l].astype(_BF16), tm_proj, 2048)
        x1, h2 = _mixer(proj, xt, conv_w[l], conv_b[l][None, :], pool_w[l].astype(_BF16),
                        pool_scale[l][None, :], w_out[l].astype(_BF16), norm_ffn[l][None, :],
                        seq, tm_mix)
        sk = peer_sub_keys[l].reshape(2 * PEER_HEADS, PEER_N_KEYS, PEER_HALF).astype(_BF16)
        eidx_t, gates_t = _retrieval(h2, peer_w_q[l].astype(_BF16), sk, tm_ret)
        uv = _pack_experts(peer_u[l], peer_v[l], 256)
        xt = _peer(eidx_t.T.reshape(-1), h2, x1, gates_t, norm_final[None, :], uv, tt)
    return xt.reshape(bsz, seq, d)
```

```python
import functools

import jax
import jax.numpy as jnp
from jax import lax
from jax.experimental import pallas as pl
from jax.experimental.pallas import tpu as pltpu

CONV_K = 3
POOL_WINDOWS = (2, 4, 8, 16)
PEER_HEADS = 8
PEER_N_KEYS = 128
PEER_TOPK = 16
PEER_HALF = 128
RMS_EPS = 1e-6

HALO = 16
LANES = 128
NEG_INF = float("-inf")

_BF16 = jnp.bfloat16
_F32 = jnp.float32
_PACKED = jnp.uint32


def _rmsnorm(xf, g):
    ms = jnp.mean(xf * xf, axis=-1, keepdims=True)
    return xf * lax.rsqrt(ms + RMS_EPS) * g


def _gelu_tanh(x):
    return 0.5 * x * (1.0 + jnp.tanh(0.7978845608028654 * (x + 0.044715 * (x * x * x))))


def _in_proj_body(x_ref, g_ref, w_ref, o_ref, h_scr):
    @pl.when(pl.program_id(1) == 0)
    def _():
        h_scr[...] = _rmsnorm(x_ref[...], g_ref[...]).astype(_BF16)

    o_ref[...] = jnp.dot(h_scr[...], w_ref[...], preferred_element_type=_F32)


def _in_proj(x2, g, w_bf16, tm, tn):
    t, d = x2.shape
    n = w_bf16.shape[1]
    return pl.pallas_call(
        _in_proj_body,
        grid=(t // tm, n // tn),
        in_specs=[
            pl.BlockSpec((tm, d), lambda i, j: (i, 0)),
            pl.BlockSpec((1, d), lambda i, j: (0, 0)),
            pl.BlockSpec((d, tn), lambda i, j: (0, j)),
        ],
        out_specs=pl.BlockSpec((tm, tn), lambda i, j: (i, j)),
        out_shape=jax.ShapeDtypeStruct((t, n), _F32),
        scratch_shapes=[pltpu.VMEM((tm, d), _BF16)],
        compiler_params=pltpu.CompilerParams(
            dimension_semantics=("arbitrary", "arbitrary"),
            vmem_limit_bytes=40 * 1024 * 1024),
        name="in_proj",
    )(x2, g, w_bf16)


def _mixer_body(proj_ref, halo_ref, x_ref, cw_ref, cb_ref, pw_ref, psc_ref, wo_ref, g2_ref,
                x1_ref, h2_ref, zs, ps, ycat, *, tm, tiles_per_seq, cw, gd):
    tile_in_seq = pl.program_id(0) % tiles_per_seq
    first = tile_in_seq == 0

    zh = halo_ref[:, cw:2 * cw] * halo_ref[:, 2 * cw:3 * cw]
    zs[0:HALO, :] = jnp.where(first, 0.0, zh)
    zs[HALO:, :] = proj_ref[:, cw:2 * cw] * proj_ref[:, 2 * cw:3 * cw]
    ps[0:HALO, :] = jnp.where(first, 0.0, halo_ref[:, 3 * cw:])
    ps[HALO:, :] = proj_ref[:, 3 * cw:]

    y = cb_ref[...]
    for k in range(CONV_K):
        off = HALO - (CONV_K - 1) + k
        y = y + cw_ref[k:k + 1, :] * zs[off:off + tm, :]
    ycat[:, 0:cw] = (proj_ref[:, 0:cw] * y).astype(_BF16)

    pos = tile_in_seq * tm + lax.broadcasted_iota(jnp.int32, (tm, 1), 0)
    for g, w in enumerate(POOL_WINDOWS):
        c0 = g * gd
        cur = ps[HALO:HALO + tm, c0:c0 + gd]
        acc = cur
        for k in range(1, w):
            acc = acc + ps[HALO - k:HALO - k + tm, c0:c0 + gd]
        count = jnp.minimum(pos + 1, w).astype(_F32)
        pooled = acc / count - cur
        yp = jnp.dot(pooled.astype(_BF16), pw_ref[g], preferred_element_type=_F32)
        ycat[:, cw + c0:cw + c0 + gd] = (yp * psc_ref[:, c0:c0 + gd]).astype(_BF16)

    x1 = x_ref[...] + jnp.dot(ycat[...], wo_ref[...], preferred_element_type=_F32)
    x1_ref[...] = x1
    h2_ref[...] = _rmsnorm(x1, g2_ref[...])


def _mixer(proj, x2, conv_w, conv_b, pool_w_bf16, pool_scale, w_out_bf16, g2, seq, tm):
    t, d = x2.shape
    n = proj.shape[1]
    cw = conv_w.shape[1]
    gd = pool_w_bf16.shape[1]
    hb = tm // HALO
    body = functools.partial(_mixer_body, tm=tm, tiles_per_seq=seq // tm, cw=cw, gd=gd)
    return pl.pallas_call(
        body,
        grid=(t // tm,),
        in_specs=[
            pl.BlockSpec((tm, n), lambda i: (i, 0)),
            pl.BlockSpec((HALO, n), lambda i: (jnp.maximum(i * hb - 1, 0), 0)),
            pl.BlockSpec((tm, d), lambda i: (i, 0)),
            pl.BlockSpec((CONV_K, cw), lambda i: (0, 0)),
            pl.BlockSpec((1, cw), lambda i: (0, 0)),
            pl.BlockSpec((len(POOL_WINDOWS), gd, gd), lambda i: (0, 0, 0)),
            pl.BlockSpec((1, cw), lambda i: (0, 0)),
            pl.BlockSpec((d, d), lambda i: (0, 0)),
            pl.BlockSpec((1, d), lambda i: (0, 0)),
        ],
        out_specs=[pl.BlockSpec((tm, d), lambda i: (i, 0)),
                   pl.BlockSpec((tm, d), lambda i: (i, 0))],
        out_shape=[jax.ShapeDtypeStruct((t, d), _F32), jax.ShapeDtypeStruct((t, d), _F32)],
        scratch_shapes=[pltpu.VMEM((HALO + tm, cw), _F32),
                        pltpu.VMEM((HALO + tm, cw), _F32),
                        pltpu.VMEM((tm, d), _BF16)],
        compiler_params=pltpu.CompilerParams(
            dimension_semantics=("arbitrary",),
            vmem_limit_bytes=52 * 1024 * 1024),
        name="mixer",
    )(proj, proj, x2, conv_w, conv_b, pool_w_bf16, pool_scale, w_out_bf16, g2)


def _topk_rows(s, k):
    n = s.shape[0]
    iota = lax.broadcasted_iota(jnp.int32, s.shape, 0)
    vals, idxs = [], []
    for _ in range(k):
        m = jnp.max(s, axis=0, keepdims=True)
        i = jnp.min(jnp.where(s == m, iota, n), axis=0, keepdims=True)
        vals.append(m)
        idxs.append(i)
        s = jnp.where(iota == i, NEG_INF, s)
    return jnp.concatenate(vals, axis=0), jnp.concatenate(idxs, axis=0)


def _select_rows(table, sel):
    out = jnp.zeros(sel.shape, table.dtype)
    for i in range(table.shape[0]):
        out = jnp.where(sel == i, table[i:i + 1, :], out)
    return out


_PAIRS = [(i, j) for i in range(PEER_TOPK) for j in range(PEER_TOPK) if (i + 1) * (j + 1) <= PEER_TOPK]
_PAIR_ROWS = -(-len(_PAIRS) // 8) * 8


def _retrieval_body(h_ref, wq_ref, sk_ref, eidx_ref, gate_ref):
    q = jnp.dot(h_ref[...].astype(_BF16), wq_ref[...], preferred_element_type=_F32)
    nt = (((1,), (1,)), ((), ()))
    tm = q.shape[0]
    pad = _PAIR_ROWS - len(_PAIRS)
    for h in range(PEER_HEADS):
        sv, si = [], []
        for p in range(2):
            hp = 2 * h + p
            qhp = q[:, hp * PEER_HALF:(hp + 1) * PEER_HALF].astype(_BF16)
            s = lax.dot_general(sk_ref[hp], qhp, nt, preferred_element_type=_F32)
            v, i = _topk_rows(s, PEER_TOPK)
            sv.append(v)
            si.append(i)
        cand = jnp.concatenate([sv[0][i:i + 1, :] + sv[1][j:j + 1, :] for i, j in _PAIRS]
                               + [jnp.full((pad, tm), NEG_INF, _F32)], axis=0)
        cidx = jnp.concatenate([si[0][i:i + 1, :] * PEER_N_KEYS + si[1][j:j + 1, :] for i, j in _PAIRS]
                               + [jnp.zeros((pad, tm), jnp.int32)], axis=0)
        top_s, top_row = _topk_rows(cand, PEER_TOPK)
        ex = jnp.exp(top_s - jnp.max(top_s, axis=0, keepdims=True))
        r0 = h * PEER_TOPK
        eidx_ref[r0:r0 + PEER_TOPK, :] = _select_rows(cidx, top_row)
        gate_ref[r0:r0 + PEER_TOPK, :] = ex / jnp.sum(ex, axis=0, keepdims=True)


def _retrieval(h2, wq_bf16, sk_bf16, tm):
    t, d = h2.shape
    hk = PEER_HEADS * PEER_TOPK
    return pl.pallas_call(
        _retrieval_body,
        grid=(t // tm,),
        in_specs=[
            pl.BlockSpec((tm, d), lambda i: (i, 0)),
            pl.BlockSpec(wq_bf16.shape, lambda i: (0, 0)),
            pl.BlockSpec(sk_bf16.shape, lambda i: (0, 0, 0)),
        ],
        out_specs=[pl.BlockSpec((hk, tm), lambda i: (0, i)),
                   pl.BlockSpec((hk, tm), lambda i: (0, i))],
        out_shape=[jax.ShapeDtypeStruct((hk, t), jnp.int32), jax.ShapeDtypeStruct((hk, t), _F32)],
        compiler_params=pltpu.CompilerParams(
            dimension_semantics=("arbitrary",),
            vmem_limit_bytes=40 * 1024 * 1024),
        name="retrieval",
    )(h2, wq_bf16, sk_bf16)


def _pack_body(u_ref, v_ref, o_ref):
    w = pltpu.pack_elementwise([u_ref[...], v_ref[...]], packed_dtype=_BF16)
    for c in range(o_ref.shape[1]):
        o_ref[:, c, :] = w[:, c * LANES:(c + 1) * LANES]


def _unpack(w, half):
    return pltpu.unpack_elementwise(w, index=half, packed_dtype=_BF16, unpacked_dtype=_F32)


def _pack_experts(u, v, rows):
    n, d = u.shape
    return pl.pallas_call(
        _pack_body,
        grid=(n // rows,),
        in_specs=[pl.BlockSpec((rows, d), lambda i: (i, 0)),
                  pl.BlockSpec((rows, d), lambda i: (i, 0))],
        out_specs=pl.BlockSpec((rows, d // LANES, LANES), lambda i: (i, 0, 0)),
        out_shape=jax.ShapeDtypeStruct((n, d // LANES, LANES), _PACKED),
        compiler_params=pltpu.CompilerParams(dimension_semantics=("arbitrary",)),
        name="pack_experts",
    )(u, v)


PEER_SLOTS = 8


def _peer_body(idx_ref, nxt_ref, h_ref, x1_ref, gate_ref, gf_ref, uv_hbm, uv_slabs, o_ref, *scratch,
               tt, hk, d):
    bufs, (y_scr, sem) = scratch[:PEER_SLOTS], scratch[PEER_SLOTS:]
    ahead = PEER_SLOTS - 1
    chunks = d // LANES
    step = pl.program_id(0)

    def issue(ids_ref, t, slot):
        for k in range(hk):
            e = ids_ref[t * hk + k]
            pltpu.make_async_copy(uv_hbm.at[e], bufs[slot].at[:, k, :],
                                  sem.at[slot]).start(priority=k % 2)

    def wait_all(slot):
        pltpu.make_async_copy(uv_slabs.at[pl.ds(0, chunks)], bufs[slot], sem.at[slot]).wait()

    lane_tok = lax.broadcasted_iota(jnp.int32, (hk, tt), 1)

    def compute(t, slot):
        x = h_ref[pl.ds(t, 1), :]
        acc = jnp.zeros((hk, LANES), _F32)
        for c in range(chunks):
            acc = acc + _unpack(bufs[slot][c], 0) * x[:, c * LANES:(c + 1) * LANES]
        a = jnp.sum(acc, axis=1, keepdims=True)
        g = jnp.sum(jnp.where(lane_tok == t, gate_ref[...], 0.0), axis=1, keepdims=True)
        coef = g * _gelu_tanh(a)
        y_scr[pl.ds(t, 1), :] = jnp.concatenate(
            [jnp.sum(coef * _unpack(bufs[slot][c], 1), axis=0, keepdims=True) for c in range(chunks)],
            axis=1)

    @pl.when(step == 0)
    def _():
        for s in range(ahead):
            issue(idx_ref, s, s)

    def steady(j, carry):
        t0 = PEER_SLOTS * j
        for s in range(PEER_SLOTS):
            wait_all(s)
            issue(idx_ref, t0 + s + ahead, (s + ahead) % PEER_SLOTS)
            compute(t0 + s, s)
        return carry

    lax.fori_loop(0, tt // PEER_SLOTS - 1, steady, 0)

    t0 = tt - PEER_SLOTS
    for s in range(PEER_SLOTS):
        wait_all(s)
        if s == 0:
            issue(idx_ref, tt - 1, ahead)
        else:
            issue(nxt_ref, s - 1, s - 1)
        compute(t0 + s, s)

    @pl.when(step + 1 == pl.num_programs(0))
    def _():
        for s in range(ahead):
            wait_all(s)

    o_ref[...] = _rmsnorm(x1_ref[...] + y_scr[...], gf_ref[...])


def _peer(eidx_flat, h2, x1, gates_t, g_final, uv, tt):
    t, d = h2.shape
    hk = gates_t.shape[0]
    body = functools.partial(_peer_body, tt=tt, hk=hk, d=d)
    last = t // tt - 1
    return pl.pallas_call(
        body,
        grid=(t // tt,),
        in_specs=[
            pl.BlockSpec((tt * hk,), lambda i: (i,), memory_space=pltpu.SMEM),
            pl.BlockSpec((tt * hk,), lambda i: (jnp.minimum(i + 1, last),), memory_space=pltpu.SMEM),
            pl.BlockSpec((tt, d), lambda i: (i, 0)),
            pl.BlockSpec((tt, d), lambda i: (i, 0)),
            pl.BlockSpec((hk, tt), lambda i: (0, i)),
            pl.BlockSpec((1, d), lambda i: (0, 0)),
            pl.BlockSpec(memory_space=pl.ANY),
            pl.BlockSpec(memory_space=pl.ANY),
        ],
        out_specs=pl.BlockSpec((tt, d), lambda i: (i, 0)),
        out_shape=jax.ShapeDtypeStruct((t, d), _F32),
        scratch_shapes=[pltpu.VMEM((d // LANES, hk, LANES), _PACKED)] * PEER_SLOTS
                       + [pltpu.VMEM((tt, d), _F32), pltpu.SemaphoreType.DMA((PEER_SLOTS,))],
        compiler_params=pltpu.CompilerParams(
            dimension_semantics=("arbitrary",),
            vmem_limit_bytes=40 * 1024 * 1024),
        name="peer",
    )(eidx_flat, eidx_flat, h2, x1, gates_t, g_final, uv, uv.reshape(-1, hk, LANES))


def kernel(x, norm_mix, w_in, conv_w, conv_b, pool_w, pool_scale, w_out, norm_ffn,
           peer_w_q, peer_sub_keys, peer_u, peer_v, norm_final):
    bsz, seq, d = x.shape
    t = bsz * seq
    depth = w_in.shape[0]
    assert depth == 1, "the final rmsnorm is fused into the single layer's peer call"
    xt = x.reshape(t, d)
    tm_proj = min(512, t)
    tm_mix = min(256, seq)
    tm_ret = min(256, t)
    tt = 128
    for l in range(depth):
        proj = _in_proj(xt, norm_mix[l][None, :], w_in[l].astype(_BF16), tm_proj, 2048)
        x1, h2 = _mixer(proj, xt, conv_w[l], conv_b[l][None, :], pool_w[l].astype(_BF16),
                        pool_scale[l][None, :], w_out[l].astype(_BF16), norm_ffn[l][None, :],
                        seq, tm_mix)
        sk = peer_sub_keys[l].reshape(2 * PEER_HEADS, PEER_N_KEYS, PEER_HALF).astype(_BF16)
        eidx_t, gates_t = _retrieval(h2, peer_w_q[l].astype(_BF16), sk, tm_ret)
        uv = _pack_experts(peer_u[l], peer_v[l], 256)
        xt = _peer(eidx_t.T.reshape(-1), h2, x1, gates_t, norm_final[None, :], uv, tt)
    return xt.reshape(bsz, seq, d)
```

```python
import functools

import jax
import jax.numpy as jnp
from jax import lax
from jax.experimental import pallas as pl
from jax.experimental.pallas import tpu as pltpu

CONV_K = 3
POOL_WINDOWS = (2, 4, 8, 16)
PEER_HEADS = 8
PEER_N_KEYS = 128
PEER_TOPK = 16
PEER_HALF = 128
RMS_EPS = 1e-6

HALO = 16
LANES = 128
NEG_INF = float("-inf")

_BF16 = jnp.bfloat16
_F32 = jnp.float32
_PACKED = jnp.uint32


def _rmsnorm(xf, g):
    ms = jnp.mean(xf * xf, axis=-1, keepdims=True)
    return xf * lax.rsqrt(ms + RMS_EPS) * g


def _gelu_tanh(x):
    return 0.5 * x * (1.0 + jnp.tanh(0.7978845608028654 * (x + 0.044715 * (x * x * x))))


def _in_proj_body(x_ref, g_ref, w_ref, o_ref):
    h = _rmsnorm(x_ref[...], g_ref[...]).astype(_BF16)
    o_ref[...] = jnp.dot(h, w_ref[...], preferred_element_type=_F32)


def _in_proj(x2, g, w_bf16, tm, tn):
    t, d = x2.shape
    n = w_bf16.shape[1]
    return pl.pallas_call(
        _in_proj_body,
        grid=(n // tn, t // tm),
        in_specs=[
            pl.BlockSpec((tm, d), lambda j, i: (i, 0)),
            pl.BlockSpec((1, d), lambda j, i: (0, 0)),
            pl.BlockSpec((d, tn), lambda j, i: (0, j)),
        ],
        out_specs=pl.BlockSpec((tm, tn), lambda j, i: (i, j)),
        out_shape=jax.ShapeDtypeStruct((t, n), _F32),
        compiler_params=pltpu.CompilerParams(
            dimension_semantics=("arbitrary", "arbitrary"),
            vmem_limit_bytes=40 * 1024 * 1024),
        name="in_proj",
    )(x2, g, w_bf16)


def _mixer_body(proj_ref, halo_ref, x_ref, cw_ref, cb_ref, pw_ref, psc_ref, wo_ref, g2_ref,
                x1_ref, h2_ref, zs, ps, ycat, *, tm, tiles_per_seq, cw, gd):
    tile_in_seq = pl.program_id(0) % tiles_per_seq
    first = tile_in_seq == 0

    zh = halo_ref[:, cw:2 * cw] * halo_ref[:, 2 * cw:3 * cw]
    zs[0:HALO, :] = jnp.where(first, 0.0, zh)
    zs[HALO:, :] = proj_ref[:, cw:2 * cw] * proj_ref[:, 2 * cw:3 * cw]
    ps[0:HALO, :] = jnp.where(first, 0.0, halo_ref[:, 3 * cw:])
    ps[HALO:, :] = proj_ref[:, 3 * cw:]

    y = cb_ref[...]
    for k in range(CONV_K):
        off = HALO - (CONV_K - 1) + k
        y = y + cw_ref[k:k + 1, :] * zs[off:off + tm, :]
    ycat[:, 0:cw] = (proj_ref[:, 0:cw] * y).astype(_BF16)

    pos = tile_in_seq * tm + lax.broadcasted_iota(jnp.int32, (tm, 1), 0)
    for g, w in enumerate(POOL_WINDOWS):
        c0 = g * gd
        cur = ps[HALO:HALO + tm, c0:c0 + gd]
        acc = cur
        for k in range(1, w):
            acc = acc + ps[HALO - k:HALO - k + tm, c0:c0 + gd]
        count = jnp.minimum(pos + 1, w).astype(_F32)
        pooled = acc / count - cur
        yp = jnp.dot(pooled.astype(_BF16), pw_ref[g], preferred_element_type=_F32)
        ycat[:, cw + c0:cw + c0 + gd] = (yp * psc_ref[:, c0:c0 + gd]).astype(_BF16)

    x1 = x_ref[...] + jnp.dot(ycat[...], wo_ref[...], preferred_element_type=_F32)
    x1_ref[...] = x1
    h2_ref[...] = _rmsnorm(x1, g2_ref[...])


def _mixer(proj, x2, conv_w, conv_b, pool_w_bf16, pool_scale, w_out_bf16, g2, seq, tm):
    t, d = x2.shape
    n = proj.shape[1]
    cw = conv_w.shape[1]
    gd = pool_w_bf16.shape[1]
    hb = tm // HALO
    body = functools.partial(_mixer_body, tm=tm, tiles_per_seq=seq // tm, cw=cw, gd=gd)
    return pl.pallas_call(
        body,
        grid=(t // tm,),
        in_specs=[
            pl.BlockSpec((tm, n), lambda i: (i, 0)),
            pl.BlockSpec((HALO, n), lambda i: (jnp.maximum(i * hb - 1, 0), 0)),
            pl.BlockSpec((tm, d), lambda i: (i, 0)),
            pl.BlockSpec((CONV_K, cw), lambda i: (0, 0)),
            pl.BlockSpec((1, cw), lambda i: (0, 0)),
            pl.BlockSpec((len(POOL_WINDOWS), gd, gd), lambda i: (0, 0, 0)),
            pl.BlockSpec((1, cw), lambda i: (0, 0)),
            pl.BlockSpec((d, d), lambda i: (0, 0)),
            pl.BlockSpec((1, d), lambda i: (0, 0)),
        ],
        out_specs=[pl.BlockSpec((tm, d), lambda i: (i, 0)),
                   pl.BlockSpec((tm, d), lambda i: (i, 0))],
        out_shape=[jax.ShapeDtypeStruct((t, d), _F32), jax.ShapeDtypeStruct((t, d), _F32)],
        scratch_shapes=[pltpu.VMEM((HALO + tm, cw), _F32),
                        pltpu.VMEM((HALO + tm, cw), _F32),
                        pltpu.VMEM((tm, d), _BF16)],
        compiler_params=pltpu.CompilerParams(
            dimension_semantics=("arbitrary",),
            vmem_limit_bytes=52 * 1024 * 1024),
        name="mixer",
    )(proj, proj, x2, conv_w, conv_b, pool_w_bf16, pool_scale, w_out_bf16, g2)


def _topk_rows(s, k):
    n = s.shape[0]
    iota = lax.broadcasted_iota(jnp.int32, s.shape, 0)
    vals, idxs = [], []
    for _ in range(k):
        m = jnp.max(s, axis=0, keepdims=True)
        i = jnp.min(jnp.where(s == m, iota, n), axis=0, keepdims=True)
        vals.append(m)
        idxs.append(i)
        s = jnp.where(iota == i, NEG_INF, s)
    return jnp.concatenate(vals, axis=0), jnp.concatenate(idxs, axis=0)


def _select_rows(table, sel):
    out = jnp.zeros(sel.shape, table.dtype)
    for i in range(table.shape[0]):
        out = jnp.where(sel == i, table[i:i + 1, :], out)
    return out


_PAIRS = [(i, j) for i in range(PEER_TOPK) for j in range(PEER_TOPK) if (i + 1) * (j + 1) <= PEER_TOPK]
_PAIR_ROWS = -(-len(_PAIRS) // 8) * 8


def _retrieval_body(h_ref, wq_ref, sk_ref, eidx_ref, gate_ref):
    q = jnp.dot(h_ref[...].astype(_BF16), wq_ref[...], preferred_element_type=_F32)
    nt = (((1,), (1,)), ((), ()))
    tm = q.shape[0]
    pad = _PAIR_ROWS - len(_PAIRS)
    for h in range(PEER_HEADS):
        sv, si = [], []
        for p in range(2):
            hp = 2 * h + p
            qhp = q[:, hp * PEER_HALF:(hp + 1) * PEER_HALF].astype(_BF16)
            s = lax.dot_general(sk_ref[hp], qhp, nt, preferred_element_type=_F32)
            v, i = _topk_rows(s, PEER_TOPK)
            sv.append(v)
            si.append(i)
        cand = jnp.concatenate([sv[0][i:i + 1, :] + sv[1][j:j + 1, :] for i, j in _PAIRS]
                               + [jnp.full((pad, tm), NEG_INF, _F32)], axis=0)
        cidx = jnp.concatenate([si[0][i:i + 1, :] * PEER_N_KEYS + si[1][j:j + 1, :] for i, j in _PAIRS]
                               + [jnp.zeros((pad, tm), jnp.int32)], axis=0)
        top_s, top_row = _topk_rows(cand, PEER_TOPK)
        ex = jnp.exp(top_s - jnp.max(top_s, axis=0, keepdims=True))
        r0 = h * PEER_TOPK
        eidx_ref[r0:r0 + PEER_TOPK, :] = _select_rows(cidx, top_row)
        gate_ref[r0:r0 + PEER_TOPK, :] = ex / jnp.sum(ex, axis=0, keepdims=True)


def _retrieval(h2, wq_bf16, sk_bf16, tm):
    t, d = h2.shape
    hk = PEER_HEADS * PEER_TOPK
    return pl.pallas_call(
        _retrieval_body,
        grid=(t // tm,),
        in_specs=[
            pl.BlockSpec((tm, d), lambda i: (i, 0)),
            pl.BlockSpec(wq_bf16.shape, lambda i: (0, 0)),
            pl.BlockSpec(sk_bf16.shape, lambda i: (0, 0, 0)),
        ],
        out_specs=[pl.BlockSpec((hk, tm), lambda i: (0, i)),
                   pl.BlockSpec((hk, tm), lambda i: (0, i))],
        out_shape=[jax.ShapeDtypeStruct((hk, t), jnp.int32), jax.ShapeDtypeStruct((hk, t), _F32)],
        compiler_params=pltpu.CompilerParams(
            dimension_semantics=("arbitrary",),
            vmem_limit_bytes=40 * 1024 * 1024),
        name="retrieval",
    )(h2, wq_bf16, sk_bf16)


def _pack_body(u_ref, v_ref, o_ref):
    w = pltpu.pack_elementwise([u_ref[...], v_ref[...]], packed_dtype=_BF16)
    for c in range(o_ref.shape[1]):
        o_ref[:, c, :] = w[:, c * LANES:(c + 1) * LANES]


def _unpack(w, half):
    return pltpu.unpack_elementwise(w, index=half, packed_dtype=_BF16, unpacked_dtype=_F32)


def _pack_experts(u, v, rows):
    n, d = u.shape
    return pl.pallas_call(
        _pack_body,
        grid=(n // rows,),
        in_specs=[pl.BlockSpec((rows, d), lambda i: (i, 0)),
                  pl.BlockSpec((rows, d), lambda i: (i, 0))],
        out_specs=pl.BlockSpec((rows, d // LANES, LANES), lambda i: (i, 0, 0)),
        out_shape=jax.ShapeDtypeStruct((n, d // LANES, LANES), _PACKED),
        compiler_params=pltpu.CompilerParams(dimension_semantics=("arbitrary",)),
        name="pack_experts",
    )(u, v)


PEER_SLOTS = 8


def _peer_body(idx_ref, nxt_ref, h_ref, x1_ref, gate_ref, gf_ref, uv_hbm, uv_slabs, o_ref, *scratch,
               tt, hk, d):
    bufs, (y_scr, sem) = scratch[:PEER_SLOTS], scratch[PEER_SLOTS:]
    ahead = PEER_SLOTS - 1
    chunks = d // LANES
    step = pl.program_id(0)

    def issue(ids_ref, t, slot):
        for k in range(hk):
            e = ids_ref[t * hk + k]
            pltpu.make_async_copy(uv_hbm.at[e], bufs[slot].at[:, k, :],
                                  sem.at[slot]).start(priority=k % 2)

    def wait_all(slot):
        pltpu.make_async_copy(uv_slabs.at[pl.ds(0, chunks)], bufs[slot], sem.at[slot]).wait()

    lane_tok = lax.broadcasted_iota(jnp.int32, (hk, tt), 1)

    def compute(t, slot):
        x = h_ref[pl.ds(t, 1), :]
        acc = jnp.zeros((hk, LANES), _F32)
        for c in range(chunks):
            acc = acc + _unpack(bufs[slot][c], 0) * x[:, c * LANES:(c + 1) * LANES]
        a = jnp.sum(acc, axis=1, keepdims=True)
        g = jnp.sum(jnp.where(lane_tok == t, gate_ref[...], 0.0), axis=1, keepdims=True)
        coef = g * _gelu_tanh(a)
        y_scr[pl.ds(t, 1), :] = jnp.concatenate(
            [jnp.sum(coef * _unpack(bufs[slot][c], 1), axis=0, keepdims=True) for c in range(chunks)],
            axis=1)

    @pl.when(step == 0)
    def _():
        for s in range(ahead):
            issue(idx_ref, s, s)

    def steady(j, carry):
        t0 = PEER_SLOTS * j
        for s in range(PEER_SLOTS):
            wait_all(s)
            issue(idx_ref, t0 + s + ahead, (s + ahead) % PEER_SLOTS)
            compute(t0 + s, s)
        return carry

    lax.fori_loop(0, tt // PEER_SLOTS - 1, steady, 0)

    t0 = tt - PEER_SLOTS
    for s in range(PEER_SLOTS):
        wait_all(s)
        if s == 0:
            issue(idx_ref, tt - 1, ahead)
        else:
            issue(nxt_ref, s - 1, s - 1)
        compute(t0 + s, s)

    @pl.when(step + 1 == pl.num_programs(0))
    def _():
        for s in range(ahead):
            wait_all(s)

    o_ref[...] = _rmsnorm(x1_ref[...] + y_scr[...], gf_ref[...])


def _peer(eidx_flat, h2, x1, gates_t, g_final, uv, tt):
    t, d = h2.shape
    hk = gates_t.shape[0]
    body = functools.partial(_peer_body, tt=tt, hk=hk, d=d)
    last = t // tt - 1
    return pl.pallas_call(
        body,
        grid=(t // tt,),
        in_specs=[
            pl.BlockSpec((tt * hk,), lambda i: (i,), memory_space=pltpu.SMEM),
            pl.BlockSpec((tt * hk,), lambda i: (jnp.minimum(i + 1, last),), memory_space=pltpu.SMEM),
            pl.BlockSpec((tt, d), lambda i: (i, 0)),
            pl.BlockSpec((tt, d), lambda i: (i, 0)),
            pl.BlockSpec((hk, tt), lambda i: (0, i)),
            pl.BlockSpec((1, d), lambda i: (0, 0)),
            pl.BlockSpec(memory_space=pl.ANY),
            pl.BlockSpec(memory_space=pl.ANY),
        ],
        out_specs=pl.BlockSpec((tt, d), lambda i: (i, 0)),
        out_shape=jax.ShapeDtypeStruct((t, d), _F32),
        scratch_shapes=[pltpu.VMEM((d // LANES, hk, LANES), _PACKED)] * PEER_SLOTS
                       + [pltpu.VMEM((tt, d), _F32), pltpu.SemaphoreType.DMA((PEER_SLOTS,))],
        compiler_params=pltpu.CompilerParams(
            dimension_semantics=("arbitrary",),
            vmem_limit_bytes=40 * 1024 * 1024),
        name="peer",
    )(eidx_flat, eidx_flat, h2, x1, gates_t, g_final, uv, uv.reshape(-1, hk, LANES))


def kernel(x, norm_mix, w_in, conv_w, conv_b, pool_w, pool_scale, w_out, norm_ffn,
           peer_w_q, peer_sub_keys, peer_u, peer_v, norm_final):
    bsz, seq, d = x.shape
    t = bsz * seq
    depth = w_in.shape[0]
    assert depth == 1, "the final rmsnorm is fused into the single layer's peer call"
    xt = x.reshape(t, d)
    tm_proj = min(512, t)
    tm_mix = min(256, seq)
    tm_ret = min(256, t)
    tt = 128
    for l in range(depth):
        proj = _in_proj(xt, norm_mix[l][None, :], w_in[l].astype(_BF16), tm_proj, 2048)
        x1, h2 = _mixer(proj, xt, conv_w[l], conv_b[l][None, :], pool_w[l].astype(_BF16),
                        pool_scale[l][None, :], w_out[l].astype(_BF16), norm_ffn[l][None, :],
                        seq, tm_mix)
        sk = peer_sub_keys[l].reshape(2 * PEER_HEADS, PEER_N_KEYS, PEER_HALF).astype(_BF16)
        eidx_t, gates_t = _retrieval(h2, peer_w_q[l].astype(_BF16), sk, tm_ret)
        uv = _pack_experts(peer_u[l], peer_v[l], 256)
        xt = _peer(eidx_t.T.reshape(-1), h2, x1, gates_t, norm_final[None, :], uv, tt)
    return xt.reshape(bsz, seq, d)
```

```python
import functools

import jax
import jax.numpy as jnp
from jax import lax
from jax.experimental import pallas as pl
from jax.experimental.pallas import tpu as pltpu

CONV_K = 3
POOL_WINDOWS = (2, 4, 8, 16)
PEER_HEADS = 8
PEER_N_KEYS = 128
PEER_TOPK = 16
PEER_HALF = 128
RMS_EPS = 1e-6

HALO = 16
LANES = 128
NEG_INF = float("-inf")

_BF16 = jnp.bfloat16
_F32 = jnp.float32
_PACKED = jnp.uint32


def _rmsnorm(xf, g):
    ms = jnp.mean(xf * xf, axis=-1, keepdims=True)
    return xf * lax.rsqrt(ms + RMS_EPS) * g


def _gelu_tanh(x):
    return 0.5 * x * (1.0 + jnp.tanh(0.7978845608028654 * (x + 0.044715 * (x * x * x))))


def _in_proj_body(x_ref, g_ref, w_ref, o_ref):
    h = _rmsnorm(x_ref[...], g_ref[...]).astype(_BF16)
    o_ref[...] = jnp.dot(h, w_ref[...], preferred_element_type=_F32)


def _in_proj(x2, g, w_bf16, tm, tn):
    t, d = x2.shape
    n = w_bf16.shape[1]
    return pl.pallas_call(
        _in_proj_body,
        grid=(n // tn, t // tm),
        in_specs=[
            pl.BlockSpec((tm, d), lambda j, i: (i, 0)),
            pl.BlockSpec((1, d), lambda j, i: (0, 0)),
            pl.BlockSpec((d, tn), lambda j, i: (0, j)),
        ],
        out_specs=pl.BlockSpec((tm, tn), lambda j, i: (i, j)),
        out_shape=jax.ShapeDtypeStruct((t, n), _F32),
        compiler_params=pltpu.CompilerParams(
            dimension_semantics=("arbitrary", "arbitrary"),
            vmem_limit_bytes=40 * 1024 * 1024),
        name="in_proj",
    )(x2, g, w_bf16)


def _mixer_body(proj_ref, halo_ref, x_ref, cw_ref, cb_ref, pw_ref, psc_ref, wo_ref,
                x1_ref, zs, ps, ycat, *, tm, tiles_per_seq, cw, gd):
    tile_in_seq = pl.program_id(0) % tiles_per_seq
    first = tile_in_seq == 0

    zh = halo_ref[:, cw:2 * cw] * halo_ref[:, 2 * cw:3 * cw]
    zs[0:HALO, :] = jnp.where(first, 0.0, zh)
    zs[HALO:, :] = proj_ref[:, cw:2 * cw] * proj_ref[:, 2 * cw:3 * cw]
    ps[0:HALO, :] = jnp.where(first, 0.0, halo_ref[:, 3 * cw:])
    ps[HALO:, :] = proj_ref[:, 3 * cw:]

    y = cb_ref[...]
    for k in range(CONV_K):
        off = HALO - (CONV_K - 1) + k
        y = y + cw_ref[k:k + 1, :] * zs[off:off + tm, :]
    ycat[:, 0:cw] = (proj_ref[:, 0:cw] * y).astype(_BF16)

    pos = tile_in_seq * tm + lax.broadcasted_iota(jnp.int32, (tm, 1), 0)
    for g, w in enumerate(POOL_WINDOWS):
        c0 = g * gd
        cur = ps[HALO:HALO + tm, c0:c0 + gd]
        acc = cur
        for k in range(1, w):
            acc = acc + ps[HALO - k:HALO - k + tm, c0:c0 + gd]
        count = jnp.minimum(pos + 1, w).astype(_F32)
        pooled = acc / count - cur
        yp = jnp.dot(pooled.astype(_BF16), pw_ref[g], preferred_element_type=_F32)
        ycat[:, cw + c0:cw + c0 + gd] = (yp * psc_ref[:, c0:c0 + gd]).astype(_BF16)

    x1_ref[...] = x_ref[...] + jnp.dot(ycat[...], wo_ref[...], preferred_element_type=_F32)


def _mixer(proj, x2, conv_w, conv_b, pool_w_bf16, pool_scale, w_out_bf16, seq, tm):
    t, d = x2.shape
    n = proj.shape[1]
    cw = conv_w.shape[1]
    gd = pool_w_bf16.shape[1]
    hb = tm // HALO
    body = functools.partial(_mixer_body, tm=tm, tiles_per_seq=seq // tm, cw=cw, gd=gd)
    return pl.pallas_call(
        body,
        grid=(t // tm,),
        in_specs=[
            pl.BlockSpec((tm, n), lambda i: (i, 0)),
            pl.BlockSpec((HALO, n), lambda i: (jnp.maximum(i * hb - 1, 0), 0)),
            pl.BlockSpec((tm, d), lambda i: (i, 0)),
            pl.BlockSpec((CONV_K, cw), lambda i: (0, 0)),
            pl.BlockSpec((1, cw), lambda i: (0, 0)),
            pl.BlockSpec((len(POOL_WINDOWS), gd, gd), lambda i: (0, 0, 0)),
            pl.BlockSpec((1, cw), lambda i: (0, 0)),
            pl.BlockSpec((d, d), lambda i: (0, 0)),
        ],
        out_specs=pl.BlockSpec((tm, d), lambda i: (i, 0)),
        out_shape=jax.ShapeDtypeStruct((t, d), _F32),
        scratch_shapes=[pltpu.VMEM((HALO + tm, cw), _F32),
                        pltpu.VMEM((HALO + tm, cw), _F32),
                        pltpu.VMEM((tm, d), _BF16)],
        compiler_params=pltpu.CompilerParams(
            dimension_semantics=("arbitrary",),
            vmem_limit_bytes=52 * 1024 * 1024),
        name="mixer",
    )(proj, proj, x2, conv_w, conv_b, pool_w_bf16, pool_scale, w_out_bf16)


def _topk_rows(s, k):
    n = s.shape[0]
    iota = lax.broadcasted_iota(jnp.int32, s.shape, 0)
    vals, idxs = [], []
    for _ in range(k):
        m = jnp.max(s, axis=0, keepdims=True)
        i = jnp.min(jnp.where(s == m, iota, n), axis=0, keepdims=True)
        vals.append(m)
        idxs.append(i)
        s = jnp.where(iota == i, NEG_INF, s)
    return jnp.concatenate(vals, axis=0), jnp.concatenate(idxs, axis=0)


def _select_rows(table, sel):
    out = jnp.zeros(sel.shape, table.dtype)
    for i in range(table.shape[0]):
        out = jnp.where(sel == i, table[i:i + 1, :], out)
    return out


_PAIRS = [(i, j) for i in range(PEER_TOPK) for j in range(PEER_TOPK) if (i + 1) * (j + 1) <= PEER_TOPK]
_PAIR_ROWS = -(-len(_PAIRS) // 8) * 8


def _retrieval_body(x1_ref, g2_ref, wq_ref, sk_ref, eidx_ref, gate_ref):
    h = _rmsnorm(x1_ref[...], g2_ref[...])
    q = jnp.dot(h.astype(_BF16), wq_ref[...], preferred_element_type=_F32)
    nt = (((1,), (1,)), ((), ()))
    tm = q.shape[0]
    pad = _PAIR_ROWS - len(_PAIRS)
    for h in range(PEER_HEADS):
        sv, si = [], []
        for p in range(2):
            hp = 2 * h + p
            qhp = q[:, hp * PEER_HALF:(hp + 1) * PEER_HALF].astype(_BF16)
            s = lax.dot_general(sk_ref[hp], qhp, nt, preferred_element_type=_F32)
            v, i = _topk_rows(s, PEER_TOPK)
            sv.append(v)
            si.append(i)
        cand = jnp.concatenate([sv[0][i:i + 1, :] + sv[1][j:j + 1, :] for i, j in _PAIRS]
                               + [jnp.full((pad, tm), NEG_INF, _F32)], axis=0)
        cidx = jnp.concatenate([si[0][i:i + 1, :] * PEER_N_KEYS + si[1][j:j + 1, :] for i, j in _PAIRS]
                               + [jnp.zeros((pad, tm), jnp.int32)], axis=0)
        top_s, top_row = _topk_rows(cand, PEER_TOPK)
        ex = jnp.exp(top_s - jnp.max(top_s, axis=0, keepdims=True))
        r0 = h * PEER_TOPK
        eidx_ref[r0:r0 + PEER_TOPK, :] = _select_rows(cidx, top_row)
        gate_ref[r0:r0 + PEER_TOPK, :] = ex / jnp.sum(ex, axis=0, keepdims=True)


def _retrieval(x1, g2, wq_bf16, sk_bf16, tm):
    t, d = x1.shape
    hk = PEER_HEADS * PEER_TOPK
    return pl.pallas_call(
        _retrieval_body,
        grid=(t // tm,),
        in_specs=[
            pl.BlockSpec((tm, d), lambda i: (i, 0)),
            pl.BlockSpec((1, d), lambda i: (0, 0)),
            pl.BlockSpec(wq_bf16.shape, lambda i: (0, 0)),
            pl.BlockSpec(sk_bf16.shape, lambda i: (0, 0, 0)),
        ],
        out_specs=[pl.BlockSpec((hk, tm), lambda i: (0, i)),
                   pl.BlockSpec((hk, tm), lambda i: (0, i))],
        out_shape=[jax.ShapeDtypeStruct((hk, t), jnp.int32), jax.ShapeDtypeStruct((hk, t), _F32)],
        compiler_params=pltpu.CompilerParams(
            dimension_semantics=("arbitrary",),
            vmem_limit_bytes=40 * 1024 * 1024),
        name="retrieval",
    )(x1, g2, wq_bf16, sk_bf16)


def _pack_body(u_ref, v_ref, o_ref):
    w = pltpu.pack_elementwise([u_ref[...], v_ref[...]], packed_dtype=_BF16)
    for c in range(o_ref.shape[1]):
        o_ref[:, c, :] = w[:, c * LANES:(c + 1) * LANES]


def _unpack(w, half):
    return pltpu.unpack_elementwise(w, index=half, packed_dtype=_BF16, unpacked_dtype=_F32)


def _pack_experts(u, v, rows):
    n, d = u.shape
    return pl.pallas_call(
        _pack_body,
        grid=(n // rows,),
        in_specs=[pl.BlockSpec((rows, d), lambda i: (i, 0)),
                  pl.BlockSpec((rows, d), lambda i: (i, 0))],
        out_specs=pl.BlockSpec((rows, d // LANES, LANES), lambda i: (i, 0, 0)),
        out_shape=jax.ShapeDtypeStruct((n, d // LANES, LANES), _PACKED),
        compiler_params=pltpu.CompilerParams(dimension_semantics=("arbitrary",)),
        name="pack_experts",
    )(u, v)


PEER_SLOTS = 8


def _peer_body(idx_ref, nxt_ref, g2_ref, x1_ref, gate_ref, gf_ref, uv_hbm, uv_slabs, o_ref, *scratch,
               tt, hk, d):
    bufs, (y_scr, h_ref, sem) = scratch[:PEER_SLOTS], scratch[PEER_SLOTS:]
    ahead = PEER_SLOTS - 1
    chunks = d // LANES
    step = pl.program_id(0)
    h_ref[...] = _rmsnorm(x1_ref[...], g2_ref[...])

    def issue(ids_ref, t, slot):
        for k in range(hk):
            e = ids_ref[t * hk + k]
            pltpu.make_async_copy(uv_hbm.at[e], bufs[slot].at[:, k, :],
                                  sem.at[slot]).start(priority=k % 2)

    def wait_all(slot):
        pltpu.make_async_copy(uv_slabs.at[pl.ds(0, chunks)], bufs[slot], sem.at[slot]).wait()

    lane_tok = lax.broadcasted_iota(jnp.int32, (hk, tt), 1)

    def compute(t, slot):
        x = h_ref[pl.ds(t, 1), :]
        acc = jnp.zeros((hk, LANES), _F32)
        for c in range(chunks):
            acc = acc + _unpack(bufs[slot][c], 0) * x[:, c * LANES:(c + 1) * LANES]
        a = jnp.sum(acc, axis=1, keepdims=True)
        g = jnp.sum(jnp.where(lane_tok == t, gate_ref[...], 0.0), axis=1, keepdims=True)
        coef = g * _gelu_tanh(a)
        y_scr[pl.ds(t, 1), :] = jnp.concatenate(
            [jnp.sum(coef * _unpack(bufs[slot][c], 1), axis=0, keepdims=True) for c in range(chunks)],
            axis=1)

    @pl.when(step == 0)
    def _():
        for s in range(ahead):
            issue(idx_ref, s, s)

    def steady(j, carry):
        t0 = PEER_SLOTS * j
        for s in range(PEER_SLOTS):
            wait_all(s)
            issue(idx_ref, t0 + s + ahead, (s + ahead) % PEER_SLOTS)
            compute(t0 + s, s)
        return carry

    lax.fori_loop(0, tt // PEER_SLOTS - 1, steady, 0)

    t0 = tt - PEER_SLOTS
    for s in range(PEER_SLOTS):
        wait_all(s)
        if s == 0:
            issue(idx_ref, tt - 1, ahead)
        else:
            issue(nxt_ref, s - 1, s - 1)
        compute(t0 + s, s)

    @pl.when(step + 1 == pl.num_programs(0))
    def _():
        for s in range(ahead):
            wait_all(s)

    o_ref[...] = _rmsnorm(x1_ref[...] + y_scr[...], gf_ref[...])


def _peer(eidx_flat, g2, x1, gates_t, g_final, uv, tt):
    t, d = x1.shape
    hk = gates_t.shape[0]
    body = functools.partial(_peer_body, tt=tt, hk=hk, d=d)
    last = t // tt - 1
    return pl.pallas_call(
        body,
        grid=(t // tt,),
        in_specs=[
            pl.BlockSpec((tt * hk,), lambda i: (i,), memory_space=pltpu.SMEM),
            pl.BlockSpec((tt * hk,), lambda i: (jnp.minimum(i + 1, last),), memory_space=pltpu.SMEM),
            pl.BlockSpec((1, d), lambda i: (0, 0)),
            pl.BlockSpec((tt, d), lambda i: (i, 0)),
            pl.BlockSpec((hk, tt), lambda i: (0, i)),
            pl.BlockSpec((1, d), lambda i: (0, 0)),
            pl.BlockSpec(memory_space=pl.ANY),
            pl.BlockSpec(memory_space=pl.ANY),
        ],
        out_specs=pl.BlockSpec((tt, d), lambda i: (i, 0)),
        out_shape=jax.ShapeDtypeStruct((t, d), _F32),
        scratch_shapes=[pltpu.VMEM((d // LANES, hk, LANES), _PACKED)] * PEER_SLOTS
                       + [pltpu.VMEM((tt, d), _F32), pltpu.VMEM((tt, d), _F32),
                          pltpu.SemaphoreType.DMA((PEER_SLOTS,))],
        compiler_params=pltpu.CompilerParams(
            dimension_semantics=("arbitrary",),
            vmem_limit_bytes=40 * 1024 * 1024),
        name="peer",
    )(eidx_flat, eidx_flat, g2, x1, gates_t, g_final, uv, uv.reshape(-1, hk, LANES))


def kernel(x, norm_mix, w_in, conv_w, conv_b, pool_w, pool_scale, w_out, norm_ffn,
           peer_w_q, peer_sub_keys, peer_u, peer_v, norm_final):
    bsz, seq, d = x.shape
    t = bsz * seq
    depth = w_in.shape[0]
    assert depth == 1, "the final rmsnorm is fused into the single layer's peer call"
    xt = x.reshape(t, d)
    tm_proj = min(512, t)
    tm_mix = min(256, seq)
    tm_ret = min(256, t)
    tt = 128
    for l in range(depth):
        proj = _in_proj(xt, norm_mix[l][None, :], w_in[l].astype(_BF16), tm_proj, 2048)
        x1 = _mixer(proj, xt, conv_w[l], conv_b[l][None, :], pool_w[l].astype(_BF16),
                    pool_scale[l][None, :], w_out[l].astype(_BF16), seq, tm_mix)
        g2 = norm_ffn[l][None, :]
        sk = peer_sub_keys[l].reshape(2 * PEER_HEADS, PEER_N_KEYS, PEER_HALF).astype(_BF16)
        eidx_t, gates_t = _retrieval(x1, g2, peer_w_q[l].astype(_BF16), sk, tm_ret)
        uv = _pack_experts(peer_u[l], peer_v[l], 256)
        xt = _peer(eidx_t.T.reshape(-1), g2, x1, gates_t, norm_final[None, :], uv, tt)
    return xt.reshape(bsz, seq, d)
```
